```python
import jax, jax.numpy as jnp
from jax import lax
import numpy as np

D_MODEL = 2048
BATCH = 1
SEQ = 8192
DEPTH = 4

HEAD_DIM = 128
GDN_HEADS = 8
ATT_HEADS = 8
GDN_WIDTH = GDN_HEADS * HEAD_DIM
ATT_WIDTH = ATT_HEADS * HEAD_DIM
MIX_WIDTH = GDN_WIDTH + ATT_WIDTH
GDN_CONV = 4
GDN_CHUNK = 64
KV_RANK = 256
IDX_HEADS = 16
IDX_DIM = 64
INDEX_TOPK = 256
Q_BLOCK = 128
ROPE_THETA = 10000.0
D_FF = 5632
FFN_CONV = 3
LN_EPS = 1e-5
RMS_EPS = 1e-6
DN_ALPHA = (2 * DEPTH) ** 0.25
DN_BETA = (8 * DEPTH) ** -0.25
PROJ_SIZES = (GDN_WIDTH, GDN_WIDTH, GDN_WIDTH, GDN_WIDTH, GDN_HEADS, GDN_HEADS,
              ATT_WIDTH, KV_RANK, IDX_HEADS * IDX_DIM, IDX_DIM, IDX_HEADS)
PROJ_WIDTH = sum(PROJ_SIZES)

kernel_name = "hymba_gdn_dsa_convffn_deepnorm"


def layer_norm(x, g, b):
    xf = x.astype(jnp.float32)
    mu = jnp.mean(xf, -1, keepdims=True)
    var = jnp.mean(jnp.square(xf - mu), -1, keepdims=True)
    return ((xf - mu) * lax.rsqrt(var + LN_EPS) * g + b).astype(x.dtype)


def rms_norm(x, g):
    xf = x.astype(jnp.float32)
    return (xf * lax.rsqrt(jnp.mean(xf * xf, -1, keepdims=True) + RMS_EPS) * g).astype(x.dtype)


def l2_normalize(x):
    xf = x.astype(jnp.float32)
    return (xf * lax.rsqrt(jnp.sum(xf * xf, -1, keepdims=True) + RMS_EPS)).astype(x.dtype)


def causal_depthwise_conv(x, w):
    k = w.shape[0]
    return lax.conv_general_dilated(
        x, w[:, None, :].astype(x.dtype), window_strides=(1,), padding=((k - 1, 0),),
        dimension_numbers=('NWC', 'WIO', 'NWC'), feature_group_count=x.shape[-1])


def rope_tables(seq, dim):
    inv = ROPE_THETA ** (-jnp.arange(0, dim, 2, dtype=jnp.float32) / dim)
    ang = jnp.arange(seq, dtype=jnp.float32)[:, None] * inv[None, :]
    ang = jnp.concatenate([ang, ang], -1)
    return jnp.cos(ang), jnp.sin(ang)


def apply_rope(x, cos, sin):
    xf = x.astype(jnp.float32)
    x1, x2 = jnp.split(xf, 2, -1)
    rot = jnp.concatenate([-x2, x1], -1)
    return (xf * cos[:, None, :] + rot * sin[:, None, :]).astype(x.dtype)


def gated_delta_rule_chunked(q, k, v, g, beta):
    b, s, h, d = q.shape
    c = GDN_CHUNK
    n = s // c

    def to_chunks(t):
        return jnp.moveaxis(t.reshape((b, n, c, h) + t.shape[3:]), 3, 1)

    qc, kc, vc = [to_chunks(t.astype(jnp.float32)) for t in (q, k, v)]
    gc = jnp.cumsum(to_chunks(g.astype(jnp.float32)), axis=-1)
    bc = to_chunks(beta.astype(jnp.float32))
    causal = jnp.tril(jnp.ones((c, c), dtype=bool))
    decay = jnp.exp(jnp.where(causal, gc[..., :, None] - gc[..., None, :], -jnp.inf))
    kkt = jnp.einsum('bhnid,bhnjd->bhnij', kc, kc)
    a_mat = jnp.eye(c, dtype=jnp.float32) + jnp.tril(bc[..., :, None] * kkt * decay, -1)
    rhs = jnp.concatenate([vc * bc[..., None], kc * (bc * jnp.exp(gc))[..., None]], -1)
    sol = lax.linalg.triangular_solve(a_mat, rhs, left_side=True, lower=True, unit_diagonal=True)
    u, w = jnp.split(sol, 2, -1)
    attn_intra = jnp.einsum('bhnid,bhnjd->bhnij', qc, kc) * decay
    q_dec = qc * jnp.exp(gc)[..., None]
    g_last = gc[..., -1]
    k_dec = kc * jnp.exp(g_last[..., None] - gc)[..., None]

    def step(state, xs):
        u_i, w_i, qd_i, kd_i, a_i, gl_i = xs
        v_new = u_i - jnp.einsum('bhcd,bhde->bhce', w_i, state)
        o_i = jnp.einsum('bhcd,bhde->bhce', qd_i, state) + jnp.einsum('bhij,bhje->bhie', a_i, v_new)
        state = state * jnp.exp(gl_i)[..., None, None] + jnp.einsum('bhcd,bhce->bhde', kd_i, v_new)
        return state, o_i

    xs = tuple(jnp.moveaxis(t, 2, 0) for t in (u, w, q_dec, k_dec, attn_intra, g_last))
    state0 = jnp.zeros((b, h, d, d), jnp.float32)
    _, o = lax.scan(step, state0, xs)
    o = jnp.moveaxis(o, 0, 2).reshape(b, h, s, d)
    return jnp.swapaxes(o, 1, 2).astype(v.dtype)


def gdn_group(q, k, v, z, a, bg, conv_w, a_log, dt_bias, norm_w):
    bsz, s, _ = q.shape
    qkv = jax.nn.silu(causal_depthwise_conv(jnp.concatenate([q, k, v], -1), conv_w))
    q, k, v = [t.reshape(bsz, s, GDN_HEADS, HEAD_DIM) for t in jnp.split(qkv, 3, -1)]
    q = l2_normalize(q) * (HEAD_DIM ** -0.5)
    k = l2_normalize(k)
    beta = jax.nn.sigmoid(bg.astype(jnp.float32))
    g = -jnp.exp(a_log.astype(jnp.float32)) * jax.nn.softplus(a.astype(jnp.float32) + dt_bias.astype(jnp.float32))
    o = gated_delta_rule_chunked(q, k, v, g, beta)
    o = rms_norm(o, norm_w) * jax.nn.silu(z.reshape(bsz, s, GDN_HEADS, HEAD_DIM))
    return o.reshape(bsz, s, GDN_WIDTH)


def dsa_group(q, c_kv, q_idx, k_idx, w_idx, kv_norm_w, w_ukv, idxk_g, idxk_b, cos, sin, icos, isin):
    bsz, s, _ = q.shape
    k, v = jnp.split(rms_norm(c_kv, kv_norm_w) @ w_ukv, 2, -1)
    q = apply_rope(q.reshape(bsz, s, ATT_HEADS, HEAD_DIM), cos, sin)
    k = apply_rope(k.reshape(bsz, s, ATT_HEADS, HEAD_DIM), cos, sin)
    v = v.reshape(bsz, s, ATT_HEADS, HEAD_DIM)
    q_idx = apply_rope(q_idx.reshape(bsz, s, IDX_HEADS, IDX_DIM), icos, isin)
    k_idx = apply_rope(layer_norm(k_idx, idxk_g, idxk_b)[:, :, None, :], icos, isin)[:, :, 0]
    w_idx = w_idx.astype(jnp.float32) * (IDX_HEADS ** -0.5)
    top_k = min(INDEX_TOPK, s // 4)
    nb = s // Q_BLOCK
    kv_cat = jnp.concatenate([k, v], -1)
    key_pos = jnp.arange(s, dtype=jnp.int32)

    def blocks(t):
        return jnp.moveaxis(t.reshape((bsz, nb, Q_BLOCK) + t.shape[2:]), 1, 0)

    def attend_block(xs):
        q_b, qi_b, wi_b, start = xs
        q_pos = start + jnp.arange(Q_BLOCK, dtype=jnp.int32)
        visible = key_pos[None, :] <= q_pos[:, None]
        dots = jnp.einsum('bthd,bsd->bths', qi_b, k_idx, preferred_element_type=jnp.float32) * (IDX_DIM ** -0.5)
        index = jnp.einsum('bths,bth->bts', jax.nn.relu(dots), wi_b)
        index = jnp.where(visible[None], index, -jnp.inf)
        _, sel = lax.top_k(index, top_k)
        kv_sel = jax.vmap(lambda kvb, ib: kvb[ib])(kv_cat, sel)
        k_sel, v_sel = jnp.split(kv_sel, 2, -1)
        logits = jnp.einsum('bthd,btkhd->bthk', q_b, k_sel, preferred_element_type=jnp.float32) * (HEAD_DIM ** -0.5)
        ok = (sel <= q_pos[None, :, None])[:, :, None, :]
        p = jax.nn.softmax(jnp.where(ok, logits, -jnp.inf), axis=-1)
        return jnp.einsum('bthk,btkhd->bthd', p.astype(v_sel.dtype), v_sel)

    starts = jnp.arange(nb, dtype=jnp.int32) * Q_BLOCK
    o = lax.map(attend_block, (blocks(q), blocks(q_idx), blocks(w_idx), starts))
    return jnp.moveaxis(o, 0, 1).reshape(bsz, s, ATT_WIDTH)


def conv_ffn(x, w_up, conv_w, conv_b, w_down):
    u = causal_depthwise_conv(x @ w_up, conv_w) + conv_b
    gate, val = jnp.split(u, 2, -1)
    return (jax.nn.silu(gate) * val) @ w_down


def setup_inputs(seed: int = 0) -> dict:
    key = jax.random.key(seed)
    ks = jax.random.split(key, 20)
    L = DEPTH
    nrm = jax.random.normal
    dt = jnp.exp(jax.random.uniform(ks[4], (L, GDN_HEADS), minval=np.log(1e-3), maxval=np.log(1e-1)))
    return {
        "x": nrm(ks[0], (BATCH, SEQ, D_MODEL), jnp.float32),
        "w_in": nrm(ks[1], (L, D_MODEL, PROJ_WIDTH), jnp.float32) * D_MODEL ** -0.5,
        "gdn_conv_w": nrm(ks[2], (L, GDN_CONV, 3 * GDN_WIDTH), jnp.float32) * GDN_CONV ** -0.5,
        "gdn_a_log": jnp.log(jax.random.uniform(ks[3], (L, GDN_HEADS), minval=1.0, maxval=16.0)),
        "gdn_dt_bias": dt + jnp.log(-jnp.expm1(-dt)),
        "gdn_norm_w": 1.0 + 0.02 * nrm(ks[5], (L, HEAD_DIM), jnp.float32),
        "kv_norm_w": 1.0 + 0.02 * nrm(ks[6], (L, KV_RANK), jnp.float32),
        "w_ukv": nrm(ks[7], (L, KV_RANK, 2 * ATT_WIDTH), jnp.float32) * KV_RANK ** -0.5,
        "idx_k_norm_g": 1.0 + 0.02 * nrm(ks[8], (L, IDX_DIM), jnp.float32),
        "idx_k_norm_b": 0.02 * nrm(ks[9], (L, IDX_DIM), jnp.float32),
        "w_out": nrm(ks[10], (L, MIX_WIDTH, D_MODEL), jnp.float32) * (MIX_WIDTH ** -0.5 * DN_BETA),
        "ln1_g": 1.0 + 0.02 * nrm(ks[11], (L, D_MODEL), jnp.float32),
        "ln1_b": 0.02 * nrm(ks[12], (L, D_MODEL), jnp.float32),
        "ffn_up": nrm(ks[13], (L, D_MODEL, 2 * D_FF), jnp.float32) * D_MODEL ** -0.5,
        "ffn_conv_w": nrm(ks[14], (L, FFN_CONV, 2 * D_FF), jnp.float32) * FFN_CONV ** -0.5,
        "ffn_conv_b": 0.02 * nrm(ks[15], (L, 2 * D_FF), jnp.float32),
        "ffn_down": nrm(ks[16], (L, D_FF, D_MODEL), jnp.float32) * (D_FF ** -0.5 * DN_BETA),
        "ln2_g": 1.0 + 0.02 * nrm(ks[17], (L, D_MODEL), jnp.float32),
        "ln2_b": 0.02 * nrm(ks[18], (L, D_MODEL), jnp.float32),
    }


def reference(x, w_in, gdn_conv_w, gdn_a_log, gdn_dt_bias, gdn_norm_w, kv_norm_w, w_ukv,
              idx_k_norm_g, idx_k_norm_b, w_out, ln1_g, ln1_b, ffn_up, ffn_conv_w, ffn_conv_b,
              ffn_down, ln2_g, ln2_b):
    s = x.shape[1]
    cos, sin = rope_tables(s, HEAD_DIM)
    icos, isin = rope_tables(s, IDX_DIM)
    split_points = [int(p) for p in np.cumsum(PROJ_SIZES)[:-1]]
    for i in range(DEPTH):
        (gq, gk, gv, gz, ga, gb, aq, ckv, iq, ik, iw) = jnp.split(x @ w_in[i], split_points, axis=-1)
        o_gdn = gdn_group(gq, gk, gv, gz, ga, gb, gdn_conv_w[i], gdn_a_log[i], gdn_dt_bias[i], gdn_norm_w[i])
        o_dsa = dsa_group(aq, ckv, iq, ik, iw, kv_norm_w[i], w_ukv[i], idx_k_norm_g[i], idx_k_norm_b[i],
                          cos, sin, icos, isin)
        y = jnp.concatenate([o_gdn, o_dsa], -1) @ w_out[i]
        x = layer_norm(DN_ALPHA * x + y, ln1_g[i], ln1_b[i])
        f = conv_ffn(x, ffn_up[i], ffn_conv_w[i], ffn_conv_b[i], ffn_down[i])
        x = layer_norm(DN_ALPHA * x + f, ln2_g[i], ln2_b[i])
    return x
```

```python
import functools

import jax
import jax.numpy as jnp
import numpy as np
from jax import lax
from jax.experimental import pallas as pl
from jax.experimental.pallas import tpu as pltpu

D_MODEL = 2048
DEPTH = 4
HEAD_DIM = 128
GDN_HEADS = 8
ATT_HEADS = 8
GDN_WIDTH = GDN_HEADS * HEAD_DIM
ATT_WIDTH = ATT_HEADS * HEAD_DIM
GDN_CONV = 4
KV_RANK = 256
IDX_HEADS = 16
IDX_DIM = 64
INDEX_TOPK = 256
ROPE_THETA = 10000.0
D_FF = 5632
FFN_CONV = 3
LN_EPS = 1e-5
RMS_EPS = 1e-6
DN_ALPHA = (2 * DEPTH) ** 0.25

BF = jnp.bfloat16
F32 = jnp.float32
HIGHEST = lax.Precision.HIGHEST

LANES = 128
SUBLANES = 8
VMEM_LIMIT_BYTES = 56 * 1024 * 1024

COL_GDN_QKV = 0
COL_GDN_Z = 3 * GDN_WIDTH
COL_ATT_Q = 4 * GDN_WIDTH
COL_IDX_Q = COL_ATT_Q + ATT_WIDTH
COL_KV = COL_IDX_Q + IDX_HEADS * IDX_DIM
COL_SMALL = COL_KV + KV_RANK
SM_IW = IDX_DIM
SM_GA = SM_IW + IDX_HEADS
SM_GB = SM_GA + GDN_HEADS
PROJ_TN = 512
PROJ_PAD = 13 * PROJ_TN

GDN_TILE = 256
INV_BASE = 16
TQ = 256
KC = 256
NEG_BIAS = -1e30


def _cparams(*sem):
    return pltpu.CompilerParams(dimension_semantics=sem, vmem_limit_bytes=VMEM_LIMIT_BYTES)


def _sigmoid(x):
    return 1.0 / (1.0 + jnp.exp(-x))


def _softplus(x):
    return jnp.maximum(x, 0.0) + jnp.log(1.0 + jnp.exp(-jnp.abs(x)))


def _dot(a, b):
    return jnp.dot(a.astype(BF), b.astype(BF), preferred_element_type=F32)


def _mm_kernel(a_ref, b_ref, o_ref):
    o_ref[...] = jnp.dot(a_ref[...], b_ref[...], preferred_element_type=F32)


def _proj(xb, w_all, layer):
    s, k = xb.shape
    n = w_all.shape[2]
    tm = min(1024, s)
    return pl.pallas_call(
        _mm_kernel,
        out_shape=jax.ShapeDtypeStruct((s, n), F32),
        grid=(n // PROJ_TN, s // tm),
        in_specs=[pl.BlockSpec((tm, k), lambda j, i: (i, 0)),
                  pl.BlockSpec((None, k, PROJ_TN), lambda j, i: (layer, 0, j))],
        out_specs=pl.BlockSpec((tm, PROJ_TN), lambda j, i: (i, j)),
        compiler_params=_cparams("parallel", "parallel"),
        name="proj",
    )(xb, w_all)


def _unit_lower_inverse(a, ri, ci):
    n = a.shape[0]
    shift = int(np.log2(INV_BASE))
    same = (ri >> shift) == (ci >> shift)
    nk = jnp.where(same, -a, 0.0)
    t = jnp.where(ri == ci, 1.0, 0.0) + nk
    for _ in range(shift - 1):
        nk = _dot(nk, nk)
        t = t + _dot(t, nk)
    size = INV_BASE
    while size < n:
        wider = (ri >> (shift + 1)) == (ci >> (shift + 1))
        e = jnp.where(jnp.logical_and(wider, jnp.logical_not(same)), a, 0.0)
        t = t - _dot(t, _dot(e, t))
        same = wider
        shift += 1
        size *= 2
    return t


def _gdn_kernel(qkv_ref, z_ref, sm_ref, cw_ref, arow_ref, acol_ref, drow_ref, dcol_ref, nw_ref,
                o_ref, carry_ref, state_ref):
    n = GDN_TILE

    @pl.when(pl.program_id(0) == 0)
    def _init():
        carry_ref[...] = jnp.zeros_like(carry_ref)
        state_ref[...] = jnp.zeros_like(state_ref)

    ri = lax.broadcasted_iota(jnp.int32, (n, n), 0)
    ci = lax.broadcasted_iota(jnp.int32, (n, n), 1)
    lower = ci <= ri
    strict = ci < ri
    row8 = lax.broadcasted_iota(jnp.int32, (SUBLANES, LANES), 0)

    sm = sm_ref[...]
    g_cols = -jnp.exp(arow_ref[...]) * _softplus(sm + drow_ref[...])
    beta_cols = _sigmoid(sm)
    sm_t = sm.T
    ga = slice(SM_GA, SM_GA + GDN_HEADS)
    g_rows = -jnp.exp(acol_ref[ga, :]) * _softplus(sm_t[ga, :] + dcol_ref[ga, :])
    gc_cols = jnp.dot(jnp.where(lower, 1.0, 0.0), g_cols, precision=HIGHEST,
                      preferred_element_type=F32)
    gc_rows = jnp.dot(g_rows, jnp.where(ri <= ci, 1.0, 0.0), precision=HIGHEST,
                      preferred_element_type=F32)

    def conv_silu(off):
        x = qkv_ref[:, off:off + LANES]
        prev = carry_ref[:, off:off + LANES]
        w = cw_ref[:, off:off + LANES]
        acc = x * w[GDN_CONV - 1:GDN_CONV, :]
        for s in range(1, GDN_CONV):
            xs = pltpu.roll(x, s, axis=0)
            top = jnp.where(row8 < s, pltpu.roll(prev, s, axis=0), xs[0:SUBLANES, :])
            xs = jnp.concatenate([top, xs[SUBLANES:, :]], axis=0)
            acc = acc + xs * w[GDN_CONV - 1 - s:GDN_CONV - s, :]
        return acc * _sigmoid(acc)

    for h in range(GDN_HEADS):
        q = conv_silu(h * HEAD_DIM)
        k = conv_silu(GDN_WIDTH + h * HEAD_DIM)
        v = conv_silu(2 * GDN_WIDTH + h * HEAD_DIM)
        q = q * lax.rsqrt(jnp.sum(q * q, -1, keepdims=True) + RMS_EPS) * (HEAD_DIM ** -0.5)
        k = k * lax.rsqrt(jnp.sum(k * k, -1, keepdims=True) + RMS_EPS)
        gcol = gc_cols[:, SM_GA + h:SM_GA + h + 1]
        grow = gc_rows[h:h + 1, :]
        bcol = beta_cols[:, SM_GB + h:SM_GB + h + 1]
        glast = gcol[n - 1:n, :]
        decay = jnp.where(lower, jnp.exp(jnp.minimum(gcol - grow, 0.0)), 0.0)
        ecol = jnp.exp(gcol)
        k_t = k.T
        k_tb = k_t.astype(BF)
        kk = jnp.dot(k.astype(BF), k_tb, preferred_element_type=F32)
        qk = jnp.dot(q.astype(BF), k_tb, preferred_element_type=F32)
        a = jnp.where(strict, bcol * kk * decay, 0.0)
        t = _unit_lower_inverse(a, ri, ci)
        rhs = jnp.concatenate([v * bcol, k * (bcol * ecol)], axis=1)
        sol = _dot(t, rhs)
        u = sol[:, :HEAD_DIM]
        w = sol[:, HEAD_DIM:]
        state = state_ref[h]
        state_b = state.astype(BF)
        v_new = u - jnp.dot(w.astype(BF), state_b, preferred_element_type=F32)
        v_nb = v_new.astype(BF)
        o = (jnp.dot((q * ecol).astype(BF), state_b, preferred_element_type=F32)
             + jnp.dot((qk * decay).astype(BF), v_nb, preferred_element_type=F32))
        k_dec_t = k_t * jnp.exp(glast - grow)
        state_ref[h] = state * jnp.exp(glast) + jnp.dot(k_dec_t.astype(BF), v_nb,
                                                        preferred_element_type=F32)
        on = o * lax.rsqrt(jnp.mean(o * o, -1, keepdims=True) + RMS_EPS) * nw_ref[...]
        z = z_ref[:, h * HEAD_DIM:(h + 1) * HEAD_DIM]
        o_ref[:, h * HEAD_DIM:(h + 1) * HEAD_DIM] = (on * (z * _sigmoid(z))).astype(o_ref.dtype)

    carry_ref[...] = qkv_ref[n - SUBLANES:n, :]


def _gdn(p, conv_w, arow, acol, drow, dcol, norm_w, layer):
    s = p.shape[0]
    n = GDN_TILE
    lsel3 = lambda i: (layer, 0, 0)
    return pl.pallas_call(
        _gdn_kernel,
        out_shape=jax.ShapeDtypeStruct((s, GDN_WIDTH), BF),
        grid=(s // n,),
        in_specs=[pl.BlockSpec((n, 3 * GDN_WIDTH), lambda i: (i, COL_GDN_QKV // (3 * GDN_WIDTH))),
                  pl.BlockSpec((n, GDN_WIDTH), lambda i: (i, COL_GDN_Z // GDN_WIDTH)),
                  pl.BlockSpec((n, LANES), lambda i: (i, COL_SMALL // LANES)),
                  pl.BlockSpec((None, GDN_CONV, 3 * GDN_WIDTH), lsel3),
                  pl.BlockSpec((None, 1, LANES), lsel3),
                  pl.BlockSpec((None, LANES, 1), lsel3),
                  pl.BlockSpec((None, 1, LANES), lsel3),
                  pl.BlockSpec((None, LANES, 1), lsel3),
                  pl.BlockSpec((None, 1, HEAD_DIM), lsel3)],
        out_specs=pl.BlockSpec((n, GDN_WIDTH), lambda i: (i, 0)),
        scratch_shapes=[pltpu.VMEM((SUBLANES, 3 * GDN_WIDTH), F32),
                        pltpu.VMEM((GDN_HEADS, HEAD_DIM, HEAD_DIM), F32)],
        compiler_params=_cparams("arbitrary"),
        name="gdn",
    )(p, p, p, conv_w, arow, acol, drow, dcol, norm_w)


def _dsa_prep_kernel(aq_ref, iq_ref, ckv_ref, sm_ref, wk_ref, wvt_ref, kvg_ref, ig_ref, ib_ref,
                     cos_ref, sins_ref, cos_t_ref, sin_t_ref, icos_t_ref, isin_t_ref, icos_ref, isin_ref,
                     qt_ref, k_ref, vt_ref, qit_ref, ki_ref, wt_ref):
    half = HEAD_DIM // 2
    ihalf = IDX_DIM // 2

    aq_t = aq_ref[...].T
    cos_t = cos_t_ref[...]
    sin_t = sin_t_ref[...]
    for h in range(ATT_HEADS):
        x = aq_t[h * HEAD_DIM:(h + 1) * HEAD_DIM, :]
        rot = jnp.concatenate([-x[half:, :], x[:half, :]], axis=0)
        qt_ref[h] = ((x * cos_t + rot * sin_t) * (HEAD_DIM ** -0.5)).astype(qt_ref.dtype)

    iq_t = iq_ref[...].T
    icos_t = icos_t_ref[...]
    isin_t = isin_t_ref[...]
    for h in range(IDX_HEADS):
        x = iq_t[h * IDX_DIM:(h + 1) * IDX_DIM, :]
        rot = jnp.concatenate([-x[ihalf:, :], x[:ihalf, :]], axis=0)
        qit_ref[h * IDX_DIM:(h + 1) * IDX_DIM, :] = (
            (x * icos_t + rot * isin_t) * (IDX_DIM ** -0.5)).astype(qit_ref.dtype)

    c = ckv_ref[...]
    kvn = c * lax.rsqrt(jnp.mean(c * c, -1, keepdims=True) + RMS_EPS) * kvg_ref[...]
    k = jnp.dot(kvn.astype(BF), wk_ref[...], preferred_element_type=F32)
    cos = cos_ref[...]
    sins = sins_ref[...]
    for h in range(ATT_HEADS):
        x = k[:, h * HEAD_DIM:(h + 1) * HEAD_DIM]
        k_ref[h] = (x * cos + pltpu.roll(x, half, axis=1) * sins).astype(k_ref.dtype)
    v_t = jnp.dot(wvt_ref[...], kvn.T.astype(BF), preferred_element_type=F32)
    for h in range(ATT_HEADS):
        vt_ref[h, 0] = v_t[h * HEAD_DIM:(h + 1) * HEAD_DIM, :].astype(vt_ref.dtype)

    sm = sm_ref[...]
    lane = lax.broadcasted_iota(jnp.int32, sm.shape, 1)
    is_k = lane < IDX_DIM
    mu = jnp.sum(jnp.where(is_k, sm, 0.0), -1, keepdims=True) * (1.0 / IDX_DIM)
    d = jnp.where(is_k, sm - mu, 0.0)
    var = jnp.sum(d * d, -1, keepdims=True) * (1.0 / IDX_DIM)
    kin = d * lax.rsqrt(var + LN_EPS) * ig_ref[...] + ib_ref[...]
    below = pltpu.roll(kin, ihalf, axis=1)
    above = pltpu.roll(kin, LANES - ihalf, axis=1)
    rot = jnp.where((lane & (IDX_DIM - 1)) < ihalf, -above, below)
    kir = kin * icos_ref[...] + rot * isin_ref[...]
    ki_ref[...] = kir[:, :IDX_DIM].astype(ki_ref.dtype)
    wt_ref[...] = sm.T[SM_IW:SM_IW + IDX_HEADS, :] * (IDX_HEADS ** -0.5)


def _dsa_prep(p, wk, wvt, kvg, ig, ib, tabs, layer):
    s = p.shape[0]
    tp = KC
    lsel3 = lambda i: (layer, 0, 0)
    row = lambda i: (i, 0)
    col = lambda i: (0, i)
    out_shape = (jax.ShapeDtypeStruct((ATT_HEADS, HEAD_DIM, s), BF),
                 jax.ShapeDtypeStruct((ATT_HEADS, s, HEAD_DIM), BF),
                 jax.ShapeDtypeStruct((ATT_HEADS, s // tp, HEAD_DIM, tp), BF),
                 jax.ShapeDtypeStruct((IDX_HEADS * IDX_DIM, s), BF),
                 jax.ShapeDtypeStruct((s, IDX_DIM), BF),
                 jax.ShapeDtypeStruct((IDX_HEADS, s), F32))
    return pl.pallas_call(
        _dsa_prep_kernel,
        out_shape=out_shape,
        grid=(s // tp,),
        in_specs=[pl.BlockSpec((tp, ATT_WIDTH), lambda i: (i, COL_ATT_Q // ATT_WIDTH)),
                  pl.BlockSpec((tp, IDX_HEADS * IDX_DIM), lambda i: (i, COL_IDX_Q // (IDX_HEADS * IDX_DIM))),
                  pl.BlockSpec((tp, KV_RANK), lambda i: (i, COL_KV // KV_RANK)),
                  pl.BlockSpec((tp, LANES), lambda i: (i, COL_SMALL // LANES)),
                  pl.BlockSpec((None, KV_RANK, ATT_WIDTH), lsel3),
                  pl.BlockSpec((None, ATT_WIDTH, KV_RANK), lsel3),
                  pl.BlockSpec((None, 1, KV_RANK), lsel3),
                  pl.BlockSpec((None, 1, LANES), lsel3),
                  pl.BlockSpec((None, 1, LANES), lsel3),
                  pl.BlockSpec((tp, HEAD_DIM), row), pl.BlockSpec((tp, HEAD_DIM), row),
                  pl.BlockSpec((HEAD_DIM, tp), col), pl.BlockSpec((HEAD_DIM, tp), col),
                  pl.BlockSpec((IDX_DIM, tp), col), pl.BlockSpec((IDX_DIM, tp), col),
                  pl.BlockSpec((tp, LANES), row), pl.BlockSpec((tp, LANES), row)],
        out_specs=(pl.BlockSpec((ATT_HEADS, HEAD_DIM, tp), lambda i: (0, 0, i)),
                   pl.BlockSpec((ATT_HEADS, tp, HEAD_DIM), lambda i: (0, i, 0)),
                   pl.BlockSpec((ATT_HEADS, 1, HEAD_DIM, tp), lambda i: (0, i, 0, 0)),
                   pl.BlockSpec((IDX_HEADS * IDX_DIM, tp), col),
                   pl.BlockSpec((tp, IDX_DIM), row),
                   pl.BlockSpec((IDX_HEADS, tp), col)),
        compiler_params=_cparams("parallel"),
        name="dsa_prep",
    )(p, p, p, p, wk, wvt, kvg, ig, ib, *tabs)


def _dsa_kernel(qit_ref, wt_ref, ki_ref, qt_ref, k_ref, vt_ref, o_ref, sc_ref, *, top_k):
    i = pl.program_id(0)
    n_chunks = i + 1
    rowi = lax.broadcasted_iota(jnp.int32, (KC, TQ), 0)
    qpos = i * TQ + lax.broadcasted_iota(jnp.int32, (KC, TQ), 1)

    def chunk(c):
        return pl.ds(pl.multiple_of(c * KC, KC), KC)

    w_t = wt_ref[...]

    def score_body(c, carry):
        kic = ki_ref[chunk(c), :]
        acc = jnp.zeros((KC, TQ), F32)
        for h in range(IDX_HEADS):
            d = jnp.dot(kic, qit_ref[h * IDX_DIM:(h + 1) * IDX_DIM, :], preferred_element_type=F32)
            acc = acc + w_t[h:h + 1, :] * jnp.maximum(d, 0.0)
        visible = (c * KC + rowi) <= qpos
        sc_ref[chunk(c), :] = jnp.where(visible, acc, jnp.nan)
        return carry

    lax.fori_loop(0, n_chunks, score_body, 0)

    def count(pred):
        def body(c, cnt):
            one = jnp.where(pred(sc_ref[chunk(c), :]), 1.0, 0.0)
            return cnt + jnp.sum(one.reshape(KC // SUBLANES, SUBLANES, TQ), axis=0)
        cnt = lax.fori_loop(0, n_chunks, body, jnp.zeros((SUBLANES, TQ), F32))
        return jnp.sum(cnt, axis=0, keepdims=True)

    def key_to_f32(cu):
        ks = cu ^ jnp.int32(-2 ** 31)
        bits = jnp.where(ks < 0, ks ^ jnp.int32(2 ** 31 - 1), ks)
        return lax.bitcast_convert_type(bits, F32)

    def bit_body(it, cu):
        cand = cu | lax.shift_left(jnp.int32(1), 31 - it)
        thr_c = key_to_f32(cand)
        cnt = count(lambda x: x >= thr_c)
        return jnp.where(cnt >= top_k, cand, cu)

    cu = lax.fori_loop(0, 32, bit_body, jnp.zeros((1, TQ), jnp.int32))
    thr = jnp.where((cu & jnp.int32(-2 ** 23)) == 0, -jnp.inf, key_to_f32(cu))
    n_ge = count(lambda x: x >= thr)
    has_tie = jnp.max(n_ge) > top_k

    @pl.when(jnp.logical_not(has_tie))
    def _select():
        def body(c, carry):
            sc_ref[chunk(c), :] = jnp.where(sc_ref[chunk(c), :] >= thr, 0.0, NEG_BIAS)
            return carry
        lax.fori_loop(0, n_chunks, body, 0)

    @pl.when(has_tie)
    def _select_ties():
        need = top_k - count(lambda x: x > thr)
        ltri = jnp.where(lax.broadcasted_iota(jnp.int32, (KC, KC), 1)
                         <= lax.broadcasted_iota(jnp.int32, (KC, KC), 0), 1.0, 0.0).astype(BF)

        def body(c, seen):
            x = sc_ref[chunk(c), :]
            eq = x == thr
            rank = jnp.dot(ltri, jnp.where(eq, 1.0, 0.0).astype(BF), preferred_element_type=F32) + seen
            keep = jnp.logical_or(x > thr, jnp.logical_and(eq, rank <= need))
            sc_ref[chunk(c), :] = jnp.where(keep, 0.0, NEG_BIAS)
            return rank[KC - 1:KC, :]
        lax.fori_loop(0, n_chunks, body, jnp.zeros((1, TQ), F32))

    for h in range(ATT_HEADS):
        qh = qt_ref[h]

        def body(c, st, h=h, qh=qh):
            m, l, acc = st
            s = jnp.dot(k_ref[h, chunk(c), :], qh, preferred_element_type=F32) + sc_ref[chunk(c), :]
            m_new = jnp.maximum(m, jnp.max(s, axis=0, keepdims=True))
            alpha = jnp.exp(m - m_new)
            p = jnp.exp(s - m_new)
            l = alpha * l + jnp.sum(p, axis=0, keepdims=True)
            acc = alpha * acc + jnp.dot(vt_ref[h, c], p.astype(BF), preferred_element_type=F32)
            return m_new, l, acc

        m, l, acc = lax.fori_loop(
            0, n_chunks, body,
            (jnp.full((1, TQ), -jnp.inf, F32), jnp.zeros((1, TQ), F32), jnp.zeros((HEAD_DIM, TQ), F32)))
        o_ref[:, h * HEAD_DIM:(h + 1) * HEAD_DIM] = (acc / l).T.astype(o_ref.dtype)


def _dsa(qt, k3, vt4, qit, ki, wt):
    s = ki.shape[0]
    top_k = min(INDEX_TOPK, s // 4)
    resident = pl.Buffered(1)
    return pl.pallas_call(
        functools.partial(_dsa_kernel, top_k=top_k),
        out_shape=jax.ShapeDtypeStruct((s, ATT_WIDTH), BF),
        grid=(s // TQ,),
        in_specs=[pl.BlockSpec((IDX_HEADS * IDX_DIM, TQ), lambda i: (0, i)),
                  pl.BlockSpec((IDX_HEADS, TQ), lambda i: (0, i)),
                  pl.BlockSpec((s, IDX_DIM), lambda i: (0, 0), pipeline_mode=resident),
                  pl.BlockSpec((ATT_HEADS, HEAD_DIM, TQ), lambda i: (0, 0, i)),
                  pl.BlockSpec((ATT_HEADS, s, HEAD_DIM), lambda i: (0, 0, 0), pipeline_mode=resident),
                  pl.BlockSpec((ATT_HEADS, s // KC, HEAD_DIM, KC), lambda i: (0, 0, 0, 0),
                               pipeline_mode=resident)],
        out_specs=pl.BlockSpec((TQ, ATT_WIDTH), lambda i: (i, 0)),
        scratch_shapes=[pltpu.VMEM((s, TQ), F32)],
        compiler_params=_cparams("arbitrary"),
        name="dsa",
    )(qit, wt, ki, qt, k3, vt4)


def _layer_norm(r, g, b):
    mu = jnp.mean(r, -1, keepdims=True)
    d = r - mu
    var = jnp.mean(d * d, -1, keepdims=True)
    return d * lax.rsqrt(var + LN_EPS) * g + b


def _out_ln_kernel(a1_ref, a2_ref, w1_ref, w2_ref, x_ref, g_ref, b_ref, xo_ref, xb_ref):
    y = (jnp.dot(a1_ref[...], w1_ref[...], preferred_element_type=F32)
         + jnp.dot(a2_ref[...], w2_ref[...], preferred_element_type=F32))
    o = _layer_norm(DN_ALPHA * x_ref[...] + y, g_ref[...], b_ref[...])
    xo_ref[...] = o
    xb_ref[...] = o.astype(xb_ref.dtype)


def _out_ln(o_gdn, o_dsa, w_out, xf, g, b, layer):
    s = xf.shape[0]
    tm = min(512, s)
    row = lambda i: (i, 0)
    lsel3 = lambda i: (layer, 0, 0)
    return pl.pallas_call(
        _out_ln_kernel,
        out_shape=(jax.ShapeDtypeStruct((s, D_MODEL), F32), jax.ShapeDtypeStruct((s, D_MODEL), BF)),
        grid=(s // tm,),
        in_specs=[pl.BlockSpec((tm, GDN_WIDTH), row), pl.BlockSpec((tm, ATT_WIDTH), row),
                  pl.BlockSpec((None, GDN_WIDTH, D_MODEL), lsel3, pipeline_mode=pl.Buffered(1)),
                  pl.BlockSpec((None, ATT_WIDTH, D_MODEL), lambda i: (layer, 1, 0),
                               pipeline_mode=pl.Buffered(1)),
                  pl.BlockSpec((tm, D_MODEL), row),
                  pl.BlockSpec((None, 1, D_MODEL), lsel3), pl.BlockSpec((None, 1, D_MODEL), lsel3)],
        out_specs=(pl.BlockSpec((tm, D_MODEL), row), pl.BlockSpec((tm, D_MODEL), row)),
        compiler_params=_cparams("parallel"),
        name="out_ln",
    )(o_gdn, o_dsa, w_out, w_out, xf, g, b)


def _ffn_up_kernel(x_ref, wg_ref, wv_ref, cg_ref, cv_ref, bg_ref, bv_ref, h_ref, carry_ref):
    tm = x_ref.shape[0]

    @pl.when(pl.program_id(1) == 0)
    def _init():
        carry_ref[...] = jnp.zeros_like(carry_ref)

    row8 = lax.broadcasted_iota(jnp.int32, (SUBLANES, h_ref.shape[1]), 0)

    def conv(u, prev, w, b):
        acc = u * w[FFN_CONV - 1:FFN_CONV, :] + b
        for s in range(1, FFN_CONV):
            us = pltpu.roll(u, s, axis=0)
            top = jnp.where(row8 < s, pltpu.roll(prev, s, axis=0), us[0:SUBLANES, :])
            us = jnp.concatenate([top, us[SUBLANES:, :]], axis=0)
            acc = acc + us * w[FFN_CONV - 1 - s:FFN_CONV - s, :]
        return acc

    x = x_ref[...]
    ug = jnp.dot(x, wg_ref[...], preferred_element_type=F32)
    uv = jnp.dot(x, wv_ref[...], preferred_element_type=F32)
    gate = conv(ug, carry_ref[0], cg_ref[...], bg_ref[...])
    val = conv(uv, carry_ref[1], cv_ref[...], bv_ref[...])
    carry_ref[0] = ug[tm - SUBLANES:, :]
    carry_ref[1] = uv[tm - SUBLANES:, :]
    h_ref[...] = (gate * _sigmoid(gate) * val).astype(h_ref.dtype)


def _ffn_up(xb, w_up, conv_w, conv_b, layer):
    s = xb.shape[0]
    tm = min(512, s)
    tn = 512
    nj = D_FF // tn
    return pl.pallas_call(
        _ffn_up_kernel,
        out_shape=jax.ShapeDtypeStruct((s, D_FF), BF),
        grid=(nj, s // tm),
        in_specs=[pl.BlockSpec((tm, D_MODEL), lambda j, i: (i, 0)),
                  pl.BlockSpec((None, D_MODEL, tn), lambda j, i: (layer, 0, j)),
                  pl.BlockSpec((None, D_MODEL, tn), lambda j, i: (layer, 0, j + nj)),
                  pl.BlockSpec((None, FFN_CONV, tn), lambda j, i: (layer, 0, j)),
                  pl.BlockSpec((None, FFN_CONV, tn), lambda j, i: (layer, 0, j + nj)),
                  pl.BlockSpec((None, 1, tn), lambda j, i: (layer, 0, j)),
                  pl.BlockSpec((None, 1, tn), lambda j, i: (layer, 0, j + nj))],
        out_specs=pl.BlockSpec((tm, tn), lambda j, i: (i, j)),
        scratch_shapes=[pltpu.VMEM((2, SUBLANES, tn), F32)],
        compiler_params=_cparams("parallel", "arbitrary"),
        name="ffn_up",
    )(xb, w_up, w_up, conv_w, conv_w, conv_b, conv_b)


def _ffn_down_kernel(h_ref, w_ref, x_ref, g_ref, b_ref, xo_ref, xb_ref, acc_ref):
    kk = pl.program_id(1)

    @pl.when(kk == 0)
    def _init():
        acc_ref[...] = jnp.zeros_like(acc_ref)

    acc_ref[...] += jnp.dot(h_ref[...], w_ref[...], preferred_element_type=F32)

    @pl.when(kk == pl.num_programs(1) - 1)
    def _finish():
        o = _layer_norm(DN_ALPHA * x_ref[...] + acc_ref[...], g_ref[...], b_ref[...])
        xo_ref[...] = o
        xb_ref[...] = o.astype(xb_ref.dtype)


def _ffn_down_ln(h, w_down, xf, g, b, layer):
    s = xf.shape[0]
    tm = min(512, s)
    tk = 512
    row = lambda i, k: (i, 0)
    lsel3 = lambda i, k: (layer, 0, 0)
    return pl.pallas_call(
        _ffn_down_kernel,
        out_shape=(jax.ShapeDtypeStruct((s, D_MODEL), F32), jax.ShapeDtypeStruct((s, D_MODEL), BF)),
        grid=(s // tm, D_FF // tk),
        in_specs=[pl.BlockSpec((tm, tk), lambda i, k: (i, k)),
                  pl.BlockSpec((None, tk, D_MODEL), lambda i, k: (layer, k, 0)),
                  pl.BlockSpec((tm, D_MODEL), row),
                  pl.BlockSpec((None, 1, D_MODEL), lsel3), pl.BlockSpec((None, 1, D_MODEL), lsel3)],
        out_specs=(pl.BlockSpec((tm, D_MODEL), row), pl.BlockSpec((tm, D_MODEL), row)),
        scratch_shapes=[pltpu.VMEM((tm, D_MODEL), F32)],
        compiler_params=_cparams("parallel", "arbitrary"),
        name="ffn_down",
    )(h, w_down, xf, g, b)


def _rope_tables(seq, dim):
    inv = ROPE_THETA ** (-jnp.arange(0, dim, 2, dtype=F32) / dim)
    ang = jnp.arange(seq, dtype=F32)[:, None] * inv[None, :]
    ang = jnp.concatenate([ang, ang], -1)
    return jnp.cos(ang), jnp.sin(ang)


def _lane_pad(v, offset):
    out = jnp.zeros((v.shape[0], 1, LANES), F32)
    return out.at[:, 0, offset:offset + v.shape[1]].set(v.astype(F32))


def kernel(x, w_in, gdn_conv_w, gdn_a_log, gdn_dt_bias, gdn_norm_w, kv_norm_w, w_ukv, idx_k_norm_g,
           idx_k_norm_b, w_out, ln1_g, ln1_b, ffn_up, ffn_conv_w, ffn_conv_b, ffn_down, ln2_g, ln2_b):
    seq = x.shape[1]
    depth = w_in.shape[0]

    o_ga = 4 * GDN_WIDTH
    o_aq = o_ga + 2 * GDN_HEADS
    o_ckv = o_aq + ATT_WIDTH
    o_iq = o_ckv + KV_RANK
    o_ik = o_iq + IDX_HEADS * IDX_DIM
    o_iw = o_ik + IDX_DIM
    o_end = o_iw + IDX_HEADS
    w_in_p = jnp.concatenate(
        [w_in[:, :, :o_ga], w_in[:, :, o_aq:o_ckv], w_in[:, :, o_iq:o_ik], w_in[:, :, o_ckv:o_iq],
         w_in[:, :, o_ik:o_end], w_in[:, :, o_ga:o_aq],
         jnp.zeros((depth, D_MODEL, PROJ_PAD - o_end), w_in.dtype)], axis=-1).astype(BF)
    wk = w_ukv[:, :, :ATT_WIDTH].astype(BF)
    wvt = jnp.swapaxes(w_ukv[:, :, ATT_WIDTH:], 1, 2).astype(BF)
    w_out_b = w_out.astype(BF)
    w_up_b = ffn_up.astype(BF)
    w_down_b = ffn_down.astype(BF)

    arow = _lane_pad(gdn_a_log, SM_GA)
    drow = _lane_pad(gdn_dt_bias, SM_GA)
    acol = jnp.swapaxes(arow, 1, 2)
    dcol = jnp.swapaxes(drow, 1, 2)
    ig = _lane_pad(idx_k_norm_g, 0)
    ib = _lane_pad(idx_k_norm_b, 0)
    r3 = lambda v: v.astype(F32)[:, None, :]

    cos, sin = _rope_tables(seq, HEAD_DIM)
    icos, isin = _rope_tables(seq, IDX_DIM)
    sign = jnp.where(jnp.arange(HEAD_DIM) < HEAD_DIM // 2, -1.0, 1.0).astype(F32)
    lane_fill = lambda t: jnp.concatenate([t, jnp.zeros_like(t)], axis=-1)
    tabs = (cos, sin * sign, cos.T, sin.T, icos.T, isin.T, lane_fill(icos), lane_fill(isin))

    xf = x[0]
    xb = xf.astype(BF)
    for layer in range(depth):
        p = _proj(xb, w_in_p, layer)
        o_gdn = _gdn(p, gdn_conv_w, arow, acol, drow, dcol, r3(gdn_norm_w), layer)
        qt, k3, vt4, qit, ki, wt = _dsa_prep(p, wk, wvt, r3(kv_norm_w), ig, ib, tabs, layer)
        o_dsa = _dsa(qt, k3, vt4, qit, ki, wt)
        xf, xb = _out_ln(o_gdn, o_dsa, w_out_b, xf, r3(ln1_g), r3(ln1_b), layer)
        h = _ffn_up(xb, w_up_b, ffn_conv_w, r3(ffn_conv_b), layer)
        xf, xb = _ffn_down_ln(h, w_down_b, xf, r3(ln2_g), r3(ln2_b), layer)
    return xf[None]
```

```python
import functools

import jax
import jax.numpy as jnp
import numpy as np
from jax import lax
from jax.experimental import pallas as pl
from jax.experimental.pallas import tpu as pltpu

D_MODEL = 2048
DEPTH = 4
HEAD_DIM = 128
GDN_HEADS = 8
ATT_HEADS = 8
GDN_WIDTH = GDN_HEADS * HEAD_DIM
ATT_WIDTH = ATT_HEADS * HEAD_DIM
GDN_CONV = 4
KV_RANK = 256
IDX_HEADS = 16
IDX_DIM = 64
INDEX_TOPK = 256
ROPE_THETA = 10000.0
D_FF = 5632
FFN_CONV = 3
LN_EPS = 1e-5
RMS_EPS = 1e-6
DN_ALPHA = (2 * DEPTH) ** 0.25

BF = jnp.bfloat16
F32 = jnp.float32
HIGHEST = lax.Precision.HIGHEST

LANES = 128
SUBLANES = 8
VMEM_LIMIT_BYTES = 56 * 1024 * 1024

COL_GDN_QKV = 0
COL_GDN_Z = 3 * GDN_WIDTH
COL_ATT_Q = 4 * GDN_WIDTH
COL_IDX_Q = COL_ATT_Q + ATT_WIDTH
COL_KV = COL_IDX_Q + IDX_HEADS * IDX_DIM
COL_SMALL = COL_KV + KV_RANK
SM_IW = IDX_DIM
SM_GA = SM_IW + IDX_HEADS
SM_GB = SM_GA + GDN_HEADS
PROJ_TN = 512
PROJ_PAD = 13 * PROJ_TN

GDN_TILE = 256
INV_BASE = 16
TQ = 256
KC = 256
NEG_BIAS = -1e30
V_ROWS = HEAD_DIM + 16
LOG2E = 1.4426950408889634


def _cparams(*sem):
    return pltpu.CompilerParams(dimension_semantics=sem, vmem_limit_bytes=VMEM_LIMIT_BYTES)


def _sigmoid(x):
    return 1.0 / (1.0 + jnp.exp(-x))


def _softplus(x):
    return jnp.maximum(x, 0.0) + jnp.log(1.0 + jnp.exp(-jnp.abs(x)))


def _dot(a, b):
    return jnp.dot(a.astype(BF), b.astype(BF), preferred_element_type=F32)


def _mm_kernel(a_ref, b_ref, o_ref):
    o_ref[...] = jnp.dot(a_ref[...], b_ref[...], preferred_element_type=F32)


def _proj(xb, w_all, layer):
    s, k = xb.shape
    n = w_all.shape[2]
    tm = min(1024, s)
    return pl.pallas_call(
        _mm_kernel,
        out_shape=jax.ShapeDtypeStruct((s, n), F32),
        grid=(n // PROJ_TN, s // tm),
        in_specs=[pl.BlockSpec((tm, k), lambda j, i: (i, 0)),
                  pl.BlockSpec((None, k, PROJ_TN), lambda j, i: (layer, 0, j))],
        out_specs=pl.BlockSpec((tm, PROJ_TN), lambda j, i: (i, j)),
        compiler_params=_cparams("parallel", "parallel"),
        name="proj",
    )(xb, w_all)


M_LOWER, M_STRICT, M_EYE, M_BASE, M_MERGE0 = 0, 1, 2, 3, 4
N_MERGE = int(np.log2(GDN_TILE // INV_BASE))


def _gdn_masks():
    r = np.arange(GDN_TILE)[:, None]
    c = np.arange(GDN_TILE)[None, :]
    same = lambda size: (r // size) == (c // size)
    rows = [c <= r, c < r, c == r, same(INV_BASE)]
    rows += [same(2 * INV_BASE << i) & ~same(INV_BASE << i) for i in range(N_MERGE)]
    masks = jnp.asarray(np.stack(rows).astype(np.float32))
    tri = jnp.asarray(np.stack([c <= r, r <= c]).astype(np.float32)).astype(BF)
    return masks, tri


def _split3(x):
    h1 = x.astype(BF)
    r1 = x - h1.astype(F32)
    h2 = r1.astype(BF)
    return h1, h2, (r1 - h2.astype(F32)).astype(BF)


def _unit_lower_inverses(a, mask_ref):
    nk = [-(x * mask_ref[M_BASE]) for x in a]
    t = [mask_ref[M_EYE] + x for x in nk]
    for _ in range(int(np.log2(INV_BASE)) - 1):
        nk = [_dot(x, x) for x in nk]
        t = [x + _dot(x, y) for x, y in zip(t, nk)]
    for lvl in range(N_MERGE):
        et = [_dot(x * mask_ref[M_MERGE0 + lvl], y) for x, y in zip(a, t)]
        t = [x - _dot(x, y) for x, y in zip(t, et)]
    return t


def _gdn_kernel(qkv_ref, z_ref, sm_ref, cw_ref, arow_ref, acol_ref, drow_ref, dcol_ref, nw_ref, mask_ref, tri_ref,
                o_ref, carry_ref, state_ref):
    n = GDN_TILE
    heads = range(GDN_HEADS)

    @pl.when(pl.program_id(0) == 0)
    def _init():
        carry_ref[...] = jnp.zeros_like(carry_ref)
        state_ref[...] = jnp.zeros_like(state_ref)

    row8 = lax.broadcasted_iota(jnp.int32, (SUBLANES, LANES), 0)

    sm = sm_ref[...]
    g_cols = -jnp.exp(arow_ref[...]) * _softplus(sm + drow_ref[...])
    beta_cols = _sigmoid(sm)
    sm_t = sm.T
    ga = slice(SM_GA, SM_GA + GDN_HEADS)
    g_rows = -jnp.exp(acol_ref[ga, :]) * _softplus(sm_t[ga, :] + dcol_ref[ga, :])
    gc_cols = sum(jnp.dot(tri_ref[0], part, preferred_element_type=F32) for part in _split3(g_cols))
    gc_rows = sum(jnp.dot(part, tri_ref[1], preferred_element_type=F32) for part in _split3(g_rows))

    def conv_silu(off):
        x = qkv_ref[:, off:off + LANES]
        prev = carry_ref[:, off:off + LANES]
        w = cw_ref[:, off:off + LANES]
        acc = x * w[GDN_CONV - 1:GDN_CONV, :]
        for s in range(1, GDN_CONV):
            xs = pltpu.roll(x, s, axis=0)
            top = jnp.where(row8 < s, pltpu.roll(prev, s, axis=0), xs[0:SUBLANES, :])
            xs = jnp.concatenate([top, xs[SUBLANES:, :]], axis=0)
            acc = acc + xs * w[GDN_CONV - 1 - s:GDN_CONV - s, :]
        return acc * _sigmoid(acc)

    q = [conv_silu(h * HEAD_DIM) for h in heads]
    k = [conv_silu(GDN_WIDTH + h * HEAD_DIM) for h in heads]
    v = [conv_silu(2 * GDN_WIDTH + h * HEAD_DIM) for h in heads]
    q = [x * lax.rsqrt(jnp.sum(x * x, -1, keepdims=True) + RMS_EPS) * (HEAD_DIM ** -0.5) for x in q]
    k = [x * lax.rsqrt(jnp.sum(x * x, -1, keepdims=True) + RMS_EPS) for x in k]
    gcol = [gc_cols[:, SM_GA + h:SM_GA + h + 1] for h in heads]
    grow = [gc_rows[h:h + 1, :] for h in heads]
    bcol = [beta_cols[:, SM_GB + h:SM_GB + h + 1] for h in heads]
    glast = [x[n - 1:n, :] for x in gcol]
    ecol = [jnp.exp(x) for x in gcol]
    decay = [mask_ref[M_LOWER] * jnp.exp(jnp.minimum(c - r, 0.0)) for c, r in zip(gcol, grow)]
    k_t = [x.T for x in k]
    k_tb = [x.astype(BF) for x in k_t]
    kk = [jnp.dot(x.astype(BF), y, preferred_element_type=F32) for x, y in zip(k, k_tb)]
    qk = [jnp.dot(x.astype(BF), y, preferred_element_type=F32) for x, y in zip(q, k_tb)]
    a = [mask_ref[M_STRICT] * (b * x * d) for b, x, d in zip(bcol, kk, decay)]
    t = _unit_lower_inverses(a, mask_ref)
    rhs = [jnp.concatenate([v[h] * bcol[h], k[h] * (bcol[h] * ecol[h])], axis=1) for h in heads]
    sol = [_dot(x, y) for x, y in zip(t, rhs)]
    state = [state_ref[h] for h in heads]
    state_b = [x.astype(BF) for x in state]
    v_new = [x[:, :HEAD_DIM] - jnp.dot(x[:, HEAD_DIM:].astype(BF), s, preferred_element_type=F32)
             for x, s in zip(sol, state_b)]
    v_nb = [x.astype(BF) for x in v_new]
    o = [jnp.dot((q[h] * ecol[h]).astype(BF), state_b[h], preferred_element_type=F32)
         + jnp.dot((qk[h] * decay[h]).astype(BF), v_nb[h], preferred_element_type=F32) for h in heads]
    for h in heads:
        k_dec_t = k_t[h] * jnp.exp(glast[h] - grow[h])
        state_ref[h] = state[h] * jnp.exp(glast[h]) + jnp.dot(k_dec_t.astype(BF), v_nb[h],
                                                              preferred_element_type=F32)
    for h in heads:
        on = o[h] * lax.rsqrt(jnp.mean(o[h] * o[h], -1, keepdims=True) + RMS_EPS) * nw_ref[...]
        z = z_ref[:, h * HEAD_DIM:(h + 1) * HEAD_DIM]
        o_ref[:, h * HEAD_DIM:(h + 1) * HEAD_DIM] = (on * (z * _sigmoid(z))).astype(o_ref.dtype)

    carry_ref[...] = qkv_ref[n - SUBLANES:n, :]


def _gdn(p, conv_w, arow, acol, drow, dcol, norm_w, masks, tri, layer):
    s = p.shape[0]
    n = GDN_TILE
    lsel3 = lambda i: (layer, 0, 0)
    const3 = lambda i: (0, 0, 0)
    return pl.pallas_call(
        _gdn_kernel,
        out_shape=jax.ShapeDtypeStruct((s, GDN_WIDTH), BF),
        grid=(s // n,),
        in_specs=[pl.BlockSpec((n, 3 * GDN_WIDTH), lambda i: (i, COL_GDN_QKV // (3 * GDN_WIDTH))),
                  pl.BlockSpec((n, GDN_WIDTH), lambda i: (i, COL_GDN_Z // GDN_WIDTH)),
                  pl.BlockSpec((n, LANES), lambda i: (i, COL_SMALL // LANES)),
                  pl.BlockSpec((None, GDN_CONV, 3 * GDN_WIDTH), lsel3),
                  pl.BlockSpec((None, 1, LANES), lsel3),
                  pl.BlockSpec((None, LANES, 1), lsel3),
                  pl.BlockSpec((None, 1, LANES), lsel3),
                  pl.BlockSpec((None, LANES, 1), lsel3),
                  pl.BlockSpec((None, 1, HEAD_DIM), lsel3),
                  pl.BlockSpec(masks.shape, const3, pipeline_mode=pl.Buffered(1)),
                  pl.BlockSpec(tri.shape, const3, pipeline_mode=pl.Buffered(1))],
        out_specs=pl.BlockSpec((n, GDN_WIDTH), lambda i: (i, 0)),
        scratch_shapes=[pltpu.VMEM((SUBLANES, 3 * GDN_WIDTH), F32),
                        pltpu.VMEM((GDN_HEADS, HEAD_DIM, HEAD_DIM), F32)],
        compiler_params=_cparams("arbitrary"),
        name="gdn",
    )(p, p, p, conv_w, arow, acol, drow, dcol, norm_w, masks, tri)


def _dsa_prep_kernel(aq_ref, iq_ref, ckv_ref, sm_ref, wk_ref, wvt_ref, kvg_ref, ig_ref, ib_ref,
                     cos_ref, sins_ref, cos_t_ref, sin_t_ref, icos_t_ref, isin_t_ref, icos_ref, isin_ref,
                     qt_ref, k_ref, vt_ref, qit_ref, ki_ref, wt_ref):
    half = HEAD_DIM // 2
    ihalf = IDX_DIM // 2

    aq_t = aq_ref[...].T
    cos_t = cos_t_ref[...]
    sin_t = sin_t_ref[...]
    for h in range(ATT_HEADS):
        x = aq_t[h * HEAD_DIM:(h + 1) * HEAD_DIM, :]
        rot = jnp.concatenate([-x[half:, :], x[:half, :]], axis=0)
        qt_ref[h] = ((x * cos_t + rot * sin_t) * (LOG2E * HEAD_DIM ** -0.5)).astype(qt_ref.dtype)

    iq_t = iq_ref[...].T
    icos_t = icos_t_ref[...]
    isin_t = isin_t_ref[...]
    for h in range(IDX_HEADS):
        x = iq_t[h * IDX_DIM:(h + 1) * IDX_DIM, :]
        rot = jnp.concatenate([-x[ihalf:, :], x[:ihalf, :]], axis=0)
        qit_ref[h * IDX_DIM:(h + 1) * IDX_DIM, :] = (
            (x * icos_t + rot * isin_t) * (IDX_DIM ** -0.5)).astype(qit_ref.dtype)

    c = ckv_ref[...]
    kvn = c * lax.rsqrt(jnp.mean(c * c, -1, keepdims=True) + RMS_EPS) * kvg_ref[...]
    k = jnp.dot(kvn.astype(BF), wk_ref[...], preferred_element_type=F32)
    cos = cos_ref[...]
    sins = sins_ref[...]
    for h in range(ATT_HEADS):
        x = k[:, h * HEAD_DIM:(h + 1) * HEAD_DIM]
        k_ref[h] = (x * cos + pltpu.roll(x, half, axis=1) * sins).astype(k_ref.dtype)
    v_t = jnp.dot(wvt_ref[...], kvn.T.astype(BF), preferred_element_type=F32)
    for h in range(ATT_HEADS):
        vt_ref[h, 0, :HEAD_DIM, :] = v_t[h * HEAD_DIM:(h + 1) * HEAD_DIM, :].astype(vt_ref.dtype)
        vt_ref[h, 0, HEAD_DIM:, :] = jnp.ones((V_ROWS - HEAD_DIM, v_t.shape[1]), vt_ref.dtype)

    sm = sm_ref[...]
    lane = lax.broadcasted_iota(jnp.int32, sm.shape, 1)
    is_k = lane < IDX_DIM
    mu = jnp.sum(jnp.where(is_k, sm, 0.0), -1, keepdims=True) * (1.0 / IDX_DIM)
    d = jnp.where(is_k, sm - mu, 0.0)
    var = jnp.sum(d * d, -1, keepdims=True) * (1.0 / IDX_DIM)
    kin = d * lax.rsqrt(var + LN_EPS) * ig_ref[...] + ib_ref[...]
    below = pltpu.roll(kin, ihalf, axis=1)
    above = pltpu.roll(kin, LANES - ihalf, axis=1)
    rot = jnp.where((lane & (IDX_DIM - 1)) < ihalf, -above, below)
    kir = kin * icos_ref[...] + rot * isin_ref[...]
    ki_ref[...] = kir[:, :IDX_DIM].astype(ki_ref.dtype)
    wt_ref[...] = sm.T[SM_IW:SM_IW + IDX_HEADS, :] * (IDX_HEADS ** -0.5)


def _dsa_prep(p, wk, wvt, kvg, ig, ib, tabs, layer):
    s = p.shape[0]
    tp = KC
    lsel3 = lambda i: (layer, 0, 0)
    row = lambda i: (i, 0)
    col = lambda i: (0, i)
    out_shape = (jax.ShapeDtypeStruct((ATT_HEADS, HEAD_DIM, s), BF),
                 jax.ShapeDtypeStruct((ATT_HEADS, s, HEAD_DIM), BF),
                 jax.ShapeDtypeStruct((ATT_HEADS, s // tp, V_ROWS, tp), BF),
                 jax.ShapeDtypeStruct((IDX_HEADS * IDX_DIM, s), BF),
                 jax.ShapeDtypeStruct((s, IDX_DIM), BF),
                 jax.ShapeDtypeStruct((IDX_HEADS, s), F32))
    return pl.pallas_call(
        _dsa_prep_kernel,
        out_shape=out_shape,
        grid=(s // tp,),
        in_specs=[pl.BlockSpec((tp, ATT_WIDTH), lambda i: (i, COL_ATT_Q // ATT_WIDTH)),
                  pl.BlockSpec((tp, IDX_HEADS * IDX_DIM), lambda i: (i, COL_IDX_Q // (IDX_HEADS * IDX_DIM))),
                  pl.BlockSpec((tp, KV_RANK), lambda i: (i, COL_KV // KV_RANK)),
                  pl.BlockSpec((tp, LANES), lambda i: (i, COL_SMALL // LANES)),
                  pl.BlockSpec((None, KV_RANK, ATT_WIDTH), lsel3),
                  pl.BlockSpec((None, ATT_WIDTH, KV_RANK), lsel3),
                  pl.BlockSpec((None, 1, KV_RANK), lsel3),
                  pl.BlockSpec((None, 1, LANES), lsel3),
                  pl.BlockSpec((None, 1, LANES), lsel3),
                  pl.BlockSpec((tp, HEAD_DIM), row), pl.BlockSpec((tp, HEAD_DIM), row),
                  pl.BlockSpec((HEAD_DIM, tp), col), pl.BlockSpec((HEAD_DIM, tp), col),
                  pl.BlockSpec((IDX_DIM, tp), col), pl.BlockSpec((IDX_DIM, tp), col),
                  pl.BlockSpec((tp, LANES), row), pl.BlockSpec((tp, LANES), row)],
        out_specs=(pl.BlockSpec((ATT_HEADS, HEAD_DIM, tp), lambda i: (0, 0, i)),
                   pl.BlockSpec((ATT_HEADS, tp, HEAD_DIM), lambda i: (0, i, 0)),
                   pl.BlockSpec((ATT_HEADS, 1, V_ROWS, tp), lambda i: (0, i, 0, 0)),
                   pl.BlockSpec((IDX_HEADS * IDX_DIM, tp), col),
                   pl.BlockSpec((tp, IDX_DIM), row),
                   pl.BlockSpec((IDX_HEADS, tp), col)),
        compiler_params=_cparams("parallel"),
        name="dsa_prep",
    )(p, p, p, p, wk, wvt, kvg, ig, ib, *tabs)


def _dsa_kernel(qit_ref, wt_ref, ki_ref, qt_ref, k_ref, vt_ref, o_ref, sc_ref, m_ref, al_ref, acc_ref, s_ref, *,
                top_k):
    i = pl.program_id(0)
    n_chunks = i + 1
    rowi = lax.broadcasted_iota(jnp.int32, (KC, TQ), 0)
    qpos = i * TQ + lax.broadcasted_iota(jnp.int32, (KC, TQ), 1)

    def chunk(c):
        return pl.ds(pl.multiple_of(c * KC, KC), KC)

    w_t = wt_ref[...]

    def score_body(c, carry):
        kic = ki_ref[chunk(c), :]
        acc = jnp.zeros((KC, TQ), F32)
        for h in range(IDX_HEADS):
            d = jnp.dot(kic, qit_ref[h * IDX_DIM:(h + 1) * IDX_DIM, :], preferred_element_type=F32)
            acc = acc + w_t[h:h + 1, :] * jnp.maximum(d, 0.0)
        visible = (c * KC + rowi) <= qpos
        sc_ref[chunk(c), :] = jnp.where(visible, acc, jnp.nan)
        return carry

    lax.fori_loop(0, n_chunks, score_body, 0)

    def count(pred):
        def body(c, cnt):
            one = jnp.where(pred(sc_ref[chunk(c), :]), 1.0, 0.0)
            return cnt + jnp.sum(one.reshape(KC // SUBLANES, SUBLANES, TQ), axis=0)
        cnt = lax.fori_loop(0, n_chunks, body, jnp.zeros((SUBLANES, TQ), F32))
        return jnp.sum(cnt, axis=0, keepdims=True)

    def key_to_f32(cu):
        ks = cu ^ jnp.int32(-2 ** 31)
        bits = jnp.where(ks < 0, ks ^ jnp.int32(2 ** 31 - 1), ks)
        return lax.bitcast_convert_type(bits, F32)

    def bit_body(it, cu):
        cand = cu | lax.shift_left(jnp.int32(1), 31 - it)
        thr_c = key_to_f32(cand)
        cnt = count(lambda x: x >= thr_c)
        return jnp.where(cnt >= top_k, cand, cu)

    cu = lax.fori_loop(0, 32, bit_body, jnp.zeros((1, TQ), jnp.int32))
    thr = jnp.where((cu & jnp.int32(-2 ** 23)) == 0, -jnp.inf, key_to_f32(cu))
    n_ge = count(lambda x: x >= thr)
    has_tie = jnp.max(n_ge) > top_k

    @pl.when(jnp.logical_not(has_tie))
    def _select():
        def body(c, carry):
            sc_ref[chunk(c), :] = jnp.where(sc_ref[chunk(c), :] >= thr, 0.0, NEG_BIAS)
            return carry
        lax.fori_loop(0, n_chunks, body, 0)

    @pl.when(has_tie)
    def _select_ties():
        need = top_k - count(lambda x: x > thr)
        ltri = jnp.where(lax.broadcasted_iota(jnp.int32, (KC, KC), 1)
                         <= lax.broadcasted_iota(jnp.int32, (KC, KC), 0), 1.0, 0.0).astype(BF)

        def body(c, seen):
            x = sc_ref[chunk(c), :]
            eq = x == thr
            rank = jnp.dot(ltri, jnp.where(eq, 1.0, 0.0).astype(BF), preferred_element_type=F32) + seen
            keep = jnp.logical_or(x > thr, jnp.logical_and(eq, rank <= need))
            sc_ref[chunk(c), :] = jnp.where(keep, 0.0, NEG_BIAS)
            return rank[KC - 1:KC, :]
        lax.fori_loop(0, n_chunks, body, jnp.zeros((1, TQ), F32))

    m_ref[...] = jnp.full(m_ref.shape, -jnp.inf, F32)
    acc_ref[...] = jnp.zeros_like(acc_ref)

    def att_body(c, carry):
        bias = sc_ref[chunk(c), :]
        for h in range(ATT_HEADS):
            s = jnp.dot(k_ref[h, chunk(c), :], qt_ref[h], preferred_element_type=F32) + bias
            s_ref[h] = s
            m_old = m_ref[h:h + 1, :]
            m_new = jnp.maximum(m_old, jnp.max(s, axis=0, keepdims=True))
            al_ref[h:h + 1, :] = jnp.exp2(m_old - m_new)
            m_ref[h:h + 1, :] = m_new
        for h in range(ATT_HEADS):
            p = jnp.exp2(s_ref[h] - m_ref[h:h + 1, :])
            acc_ref[h] = al_ref[h:h + 1, :] * acc_ref[h] + jnp.dot(
                vt_ref[h, c], p.astype(BF), preferred_element_type=F32)
        return carry

    lax.fori_loop(0, n_chunks, att_body, 0)
    for h in range(ATT_HEADS):
        o_t = acc_ref[h, :HEAD_DIM, :] / acc_ref[h, HEAD_DIM:HEAD_DIM + 1, :]
        o_ref[:, h * HEAD_DIM:(h + 1) * HEAD_DIM] = o_t.T.astype(o_ref.dtype)


def _dsa(qt, k3, vt4, qit, ki, wt):
    s = ki.shape[0]
    top_k = min(INDEX_TOPK, s // 4)
    resident = pl.Buffered(1)
    return pl.pallas_call(
        functools.partial(_dsa_kernel, top_k=top_k),
        out_shape=jax.ShapeDtypeStruct((s, ATT_WIDTH), BF),
        grid=(s // TQ,),
        in_specs=[pl.BlockSpec((IDX_HEADS * IDX_DIM, TQ), lambda i: (0, i)),
                  pl.BlockSpec((IDX_HEADS, TQ), lambda i: (0, i)),
                  pl.BlockSpec((s, IDX_DIM), lambda i: (0, 0), pipeline_mode=resident),
                  pl.BlockSpec((ATT_HEADS, HEAD_DIM, TQ), lambda i: (0, 0, i)),
                  pl.BlockSpec((ATT_HEADS, s, HEAD_DIM), lambda i: (0, 0, 0), pipeline_mode=resident),
                  pl.BlockSpec((ATT_HEADS, s // KC, V_ROWS, KC), lambda i: (0, 0, 0, 0),
                               pipeline_mode=resident)],
        out_specs=pl.BlockSpec((TQ, ATT_WIDTH), lambda i: (i, 0)),
        scratch_shapes=[pltpu.VMEM((s, TQ), F32),
                        pltpu.VMEM((ATT_HEADS, TQ), F32),
                        pltpu.VMEM((ATT_HEADS, TQ), F32),
                        pltpu.VMEM((ATT_HEADS, V_ROWS, TQ), F32),
                        pltpu.VMEM((ATT_HEADS, KC, TQ), F32)],
        compiler_params=_cparams("arbitrary"),
        name="dsa",
    )(qit, wt, ki, qt, k3, vt4)


def _layer_norm(r, g, b):
    mu = jnp.mean(r, -1, keepdims=True)
    d = r - mu
    var = jnp.mean(d * d, -1, keepdims=True)
    return d * lax.rsqrt(var + LN_EPS) * g + b


def _out_ln_kernel(a1_ref, a2_ref, w1_ref, w2_ref, x_ref, g_ref, b_ref, xo_ref, xb_ref):
    y = (jnp.dot(a1_ref[...], w1_ref[...], preferred_element_type=F32)
         + jnp.dot(a2_ref[...], w2_ref[...], preferred_element_type=F32))
    o = _layer_norm(DN_ALPHA * x_ref[...] + y, g_ref[...], b_ref[...])
    xo_ref[...] = o
    xb_ref[...] = o.astype(xb_ref.dtype)


def _out_ln(o_gdn, o_dsa, w_out, xf, g, b, layer):
    s = xf.shape[0]
    tm = min(512, s)
    row = lambda i: (i, 0)
    lsel3 = lambda i: (layer, 0, 0)
    return pl.pallas_call(
        _out_ln_kernel,
        out_shape=(jax.ShapeDtypeStruct((s, D_MODEL), F32), jax.ShapeDtypeStruct((s, D_MODEL), BF)),
        grid=(s // tm,),
        in_specs=[pl.BlockSpec((tm, GDN_WIDTH), row), pl.BlockSpec((tm, ATT_WIDTH), row),
                  pl.BlockSpec((None, GDN_WIDTH, D_MODEL), lsel3, pipeline_mode=pl.Buffered(1)),
                  pl.BlockSpec((None, ATT_WIDTH, D_MODEL), lambda i: (layer, 1, 0),
                               pipeline_mode=pl.Buffered(1)),
                  pl.BlockSpec((tm, D_MODEL), row),
                  pl.BlockSpec((None, 1, D_MODEL), lsel3), pl.BlockSpec((None, 1, D_MODEL), lsel3)],
        out_specs=(pl.BlockSpec((tm, D_MODEL), row), pl.BlockSpec((tm, D_MODEL), row)),
        compiler_params=_cparams("parallel"),
        name="out_ln",
    )(o_gdn, o_dsa, w_out, w_out, xf, g, b)


def _ffn_up_kernel(x_ref, wg_ref, wv_ref, cg_ref, cv_ref, bg_ref, bv_ref, h_ref, carry_ref):
    tm = x_ref.shape[0]

    @pl.when(pl.program_id(1) == 0)
    def _init():
        carry_ref[...] = jnp.zeros_like(carry_ref)

    row8 = lax.broadcasted_iota(jnp.int32, (SUBLANES, h_ref.shape[1]), 0)

    def conv(u, prev, w, b):
        acc = u * w[FFN_CONV - 1:FFN_CONV, :] + b
        for s in range(1, FFN_CONV):
            us = pltpu.roll(u, s, axis=0)
            top = jnp.where(row8 < s, pltpu.roll(prev, s, axis=0), us[0:SUBLANES, :])
            us = jnp.concatenate([top, us[SUBLANES:, :]], axis=0)
            acc = acc + us * w[FFN_CONV - 1 - s:FFN_CONV - s, :]
        return acc

    x = x_ref[...]
    ug = jnp.dot(x, wg_ref[...], preferred_element_type=F32)
    uv = jnp.dot(x, wv_ref[...], preferred_element_type=F32)
    gate = conv(ug, carry_ref[0], cg_ref[...], bg_ref[...])
    val = conv(uv, carry_ref[1], cv_ref[...], bv_ref[...])
    carry_ref[0] = ug[tm - SUBLANES:, :]
    carry_ref[1] = uv[tm - SUBLANES:, :]
    h_ref[...] = (gate * _sigmoid(gate) * val).astype(h_ref.dtype)


def _ffn_up(xb, w_up, conv_w, conv_b, layer):
    s = xb.shape[0]
    tm = min(512, s)
    tn = 512
    nj = D_FF // tn
    return pl.pallas_call(
        _ffn_up_kernel,
        out_shape=jax.ShapeDtypeStruct((s, D_FF), BF),
        grid=(nj, s // tm),
        in_specs=[pl.BlockSpec((tm, D_MODEL), lambda j, i: (i, 0)),
                  pl.BlockSpec((None, D_MODEL, tn), lambda j, i: (layer, 0, j)),
                  pl.BlockSpec((None, D_MODEL, tn), lambda j, i: (layer, 0, j + nj)),
                  pl.BlockSpec((None, FFN_CONV, tn), lambda j, i: (layer, 0, j)),
                  pl.BlockSpec((None, FFN_CONV, tn), lambda j, i: (layer, 0, j + nj)),
                  pl.BlockSpec((None, 1, tn), lambda j, i: (layer, 0, j)),
                  pl.BlockSpec((None, 1, tn), lambda j, i: (layer, 0, j + nj))],
        out_specs=pl.BlockSpec((tm, tn), lambda j, i: (i, j)),
        scratch_shapes=[pltpu.VMEM((2, SUBLANES, tn), F32)],
        compiler_params=_cparams("parallel", "arbitrary"),
        name="ffn_up",
    )(xb, w_up, w_up, conv_w, conv_w, conv_b, conv_b)


def _ffn_down_kernel(h_ref, w_ref, x_ref, g_ref, b_ref, xo_ref, xb_ref, acc_ref):
    kk = pl.program_id(1)

    @pl.when(kk == 0)
    def _init():
        acc_ref[...] = jnp.zeros_like(acc_ref)

    acc_ref[...] += jnp.dot(h_ref[...], w_ref[...], preferred_element_type=F32)

    @pl.when(kk == pl.num_programs(1) - 1)
    def _finish():
        o = _layer_norm(DN_ALPHA * x_ref[...] + acc_ref[...], g_ref[...], b_ref[...])
        xo_ref[...] = o
        xb_ref[...] = o.astype(xb_ref.dtype)


def _ffn_down_ln(h, w_down, xf, g, b, layer):
    s = xf.shape[0]
    tm = min(512, s)
    tk = 512
    row = lambda i, k: (i, 0)
    lsel3 = lambda i, k: (layer, 0, 0)
    return pl.pallas_call(
        _ffn_down_kernel,
        out_shape=(jax.ShapeDtypeStruct((s, D_MODEL), F32), jax.ShapeDtypeStruct((s, D_MODEL), BF)),
        grid=(s // tm, D_FF // tk),
        in_specs=[pl.BlockSpec((tm, tk), lambda i, k: (i, k)),
                  pl.BlockSpec((None, tk, D_MODEL), lambda i, k: (layer, k, 0)),
                  pl.BlockSpec((tm, D_MODEL), row),
                  pl.BlockSpec((None, 1, D_MODEL), lsel3), pl.BlockSpec((None, 1, D_MODEL), lsel3)],
        out_specs=(pl.BlockSpec((tm, D_MODEL), row), pl.BlockSpec((tm, D_MODEL), row)),
        scratch_shapes=[pltpu.VMEM((tm, D_MODEL), F32)],
        compiler_params=_cparams("parallel", "arbitrary"),
        name="ffn_down",
    )(h, w_down, xf, g, b)


def _rope_tables(seq, dim):
    inv = ROPE_THETA ** (-jnp.arange(0, dim, 2, dtype=F32) / dim)
    ang = jnp.arange(seq, dtype=F32)[:, None] * inv[None, :]
    ang = jnp.concatenate([ang, ang], -1)
    return jnp.cos(ang), jnp.sin(ang)


def _lane_pad(v, offset):
    out = jnp.zeros((v.shape[0], 1, LANES), F32)
    return out.at[:, 0, offset:offset + v.shape[1]].set(v.astype(F32))


def kernel(x, w_in, gdn_conv_w, gdn_a_log, gdn_dt_bias, gdn_norm_w, kv_norm_w, w_ukv, idx_k_norm_g,
           idx_k_norm_b, w_out, ln1_g, ln1_b, ffn_up, ffn_conv_w, ffn_conv_b, ffn_down, ln2_g, ln2_b):
    seq = x.shape[1]
    depth = w_in.shape[0]

    o_ga = 4 * GDN_WIDTH
    o_aq = o_ga + 2 * GDN_HEADS
    o_ckv = o_aq + ATT_WIDTH
    o_iq = o_ckv + KV_RANK
    o_ik = o_iq + IDX_HEADS * IDX_DIM
    o_iw = o_ik + IDX_DIM
    o_end = o_iw + IDX_HEADS
    w_in_p = jnp.concatenate(
        [w_in[:, :, :o_ga], w_in[:, :, o_aq:o_ckv], w_in[:, :, o_iq:o_ik], w_in[:, :, o_ckv:o_iq],
         w_in[:, :, o_ik:o_end], w_in[:, :, o_ga:o_aq],
         jnp.zeros((depth, D_MODEL, PROJ_PAD - o_end), w_in.dtype)], axis=-1).astype(BF)
    wk = w_ukv[:, :, :ATT_WIDTH].astype(BF)
    wvt = jnp.swapaxes(w_ukv[:, :, ATT_WIDTH:], 1, 2).astype(BF)
    w_out_b = w_out.astype(BF)
    w_up_b = ffn_up.astype(BF)
    w_down_b = ffn_down.astype(BF)

    arow = _lane_pad(gdn_a_log, SM_GA)
    drow = _lane_pad(gdn_dt_bias, SM_GA)
    acol = jnp.swapaxes(arow, 1, 2)
    dcol = jnp.swapaxes(drow, 1, 2)
    ig = _lane_pad(idx_k_norm_g, 0)
    ib = _lane_pad(idx_k_norm_b, 0)
    r3 = lambda v: v.astype(F32)[:, None, :]

    cos, sin = _rope_tables(seq, HEAD_DIM)
    icos, isin = _rope_tables(seq, IDX_DIM)
    sign = jnp.where(jnp.arange(HEAD_DIM) < HEAD_DIM // 2, -1.0, 1.0).astype(F32)
    lane_fill = lambda t: jnp.concatenate([t, jnp.zeros_like(t)], axis=-1)
    tabs = (cos, sin * sign, cos.T, sin.T, icos.T, isin.T, lane_fill(icos), lane_fill(isin))

    masks, tri = _gdn_masks()

    xf = x[0]
    xb = xf.astype(BF)
    for layer in range(depth):
        p = _proj(xb, w_in_p, layer)
        o_gdn = _gdn(p, gdn_conv_w, arow, acol, drow, dcol, r3(gdn_norm_w), masks, tri, layer)
        qt, k3, vt4, qit, ki, wt = _dsa_prep(p, wk, wvt, r3(kv_norm_w), ig, ib, tabs, layer)
        o_dsa = _dsa(qt, k3, vt4, qit, ki, wt)
        xf, xb = _out_ln(o_gdn, o_dsa, w_out_b, xf, r3(ln1_g), r3(ln1_b), layer)
        h = _ffn_up(xb, w_up_b, ffn_conv_w, r3(ffn_conv_b), layer)
        xf, xb = _ffn_down_ln(h, w_down_b, xf, r3(ln2_g), r3(ln2_b), layer)
    return xf[None]
```

```python
import functools

import jax
import jax.numpy as jnp
import numpy as np
from jax import lax
from jax.experimental import pallas as pl
from jax.experimental.pallas import tpu as pltpu

D_MODEL = 2048
DEPTH = 4
HEAD_DIM = 128
GDN_HEADS = 8
ATT_HEADS = 8
GDN_WIDTH = GDN_HEADS * HEAD_DIM
ATT_WIDTH = ATT_HEADS * HEAD_DIM
GDN_CONV = 4
KV_RANK = 256
IDX_HEADS = 16
IDX_DIM = 64
INDEX_TOPK = 256
ROPE_THETA = 10000.0
D_FF = 5632
FFN_CONV = 3
LN_EPS = 1e-5
RMS_EPS = 1e-6
DN_ALPHA = (2 * DEPTH) ** 0.25

BF = jnp.bfloat16
F32 = jnp.float32
HIGHEST = lax.Precision.HIGHEST

LANES = 128
SUBLANES = 8
VMEM_LIMIT_BYTES = 56 * 1024 * 1024

MAIN_WIDTH = 4 * GDN_WIDTH
COL_GDN_QKV = 0
COL_GDN_Z = 3 * GDN_WIDTH
COL_ATT_Q = 0
COL_IDX_Q = COL_ATT_Q + ATT_WIDTH
COL_KV = COL_IDX_Q + IDX_HEADS * IDX_DIM
COL_SMALL = COL_KV + KV_RANK
SM_IW = IDX_DIM
SM_GA = SM_IW + IDX_HEADS
SM_GB = SM_GA + GDN_HEADS
TAIL_WIDTH = COL_SMALL + 2 * LANES

GDN_TILE = 256
INV_BASE = 16
TQ = 256
KC = 256
COUNT_ROWS = 64
UNTESTED_BITS = 22
NEG_BIAS = -1e30
V_ROWS = HEAD_DIM + 16
LOG2E = 1.4426950408889634


def _cparams(*sem):
    return pltpu.CompilerParams(dimension_semantics=sem, vmem_limit_bytes=VMEM_LIMIT_BYTES)


def _sigmoid(x):
    return 1.0 / (1.0 + jnp.exp(-x))


def _softplus(x):
    return jnp.maximum(x, 0.0) + jnp.log(1.0 + jnp.exp(-jnp.abs(x)))


def _dot(a, b):
    return jnp.dot(a.astype(BF), b.astype(BF), preferred_element_type=F32)


def _mm_kernel(a_ref, b_ref, o_ref):
    o_ref[...] = jnp.dot(a_ref[...], b_ref[...], preferred_element_type=F32)


def _mm_cast_kernel(a_ref, b_ref, o_ref, bb_ref):
    @pl.when(pl.program_id(1) == 0)
    def _cast():
        bb_ref[...] = b_ref[...].astype(bb_ref.dtype)

    o_ref[...] = jnp.dot(a_ref[...], bb_ref[...], preferred_element_type=F32)


def _proj_main(xb, w_in, layer):
    s, k = xb.shape
    tm = min(1024, s)
    tn = 1024
    return pl.pallas_call(
        _mm_cast_kernel,
        out_shape=jax.ShapeDtypeStruct((s, MAIN_WIDTH), F32),
        grid=(MAIN_WIDTH // tn, s // tm),
        in_specs=[pl.BlockSpec((tm, k), lambda j, i: (i, 0)),
                  pl.BlockSpec((None, k, tn), lambda j, i: (layer, 0, j))],
        out_specs=pl.BlockSpec((tm, tn), lambda j, i: (i, j)),
        scratch_shapes=[pltpu.VMEM((k, tn), BF)],
        compiler_params=_cparams("arbitrary", "arbitrary"),
        name="proj_main",
    )(xb, w_in)


def _proj_tail(xb, w_tail, layer):
    s, k = xb.shape
    tm = min(1024, s)
    tn = TAIL_WIDTH // 2
    return pl.pallas_call(
        _mm_kernel,
        out_shape=jax.ShapeDtypeStruct((s, TAIL_WIDTH), F32),
        grid=(TAIL_WIDTH // tn, s // tm),
        in_specs=[pl.BlockSpec((tm, k), lambda j, i: (i, 0)),
                  pl.BlockSpec((None, k, tn), lambda j, i: (layer, 0, j))],
        out_specs=pl.BlockSpec((tm, tn), lambda j, i: (i, j)),
        compiler_params=_cparams("parallel", "parallel"),
        name="proj_tail",
    )(xb, w_tail)


M_LOWER, M_STRICT, M_EYE, M_BASE, M_MERGE0 = 0, 1, 2, 3, 4
N_MERGE = int(np.log2(GDN_TILE // INV_BASE))


def _gdn_masks():
    r = np.arange(GDN_TILE)[:, None]
    c = np.arange(GDN_TILE)[None, :]
    same = lambda size: (r // size) == (c // size)
    rows = [c <= r, c < r, c == r, same(INV_BASE)]
    rows += [same(2 * INV_BASE << i) & ~same(INV_BASE << i) for i in range(N_MERGE)]
    masks = jnp.asarray(np.stack(rows).astype(np.float32))
    tri = jnp.asarray(np.stack([c <= r, r <= c]).astype(np.float32)).astype(BF)
    return masks, tri


def _split3(x):
    h1 = x.astype(BF)
    r1 = x - h1.astype(F32)
    h2 = r1.astype(BF)
    return h1, h2, (r1 - h2.astype(F32)).astype(BF)


def _unit_lower_inverses(a, mask_ref):
    nk = [-(x * mask_ref[M_BASE]) for x in a]
    t = [mask_ref[M_EYE] + x for x in nk]
    for _ in range(int(np.log2(INV_BASE)) - 1):
        nk = [_dot(x, x) for x in nk]
        t = [x + _dot(x, y) for x, y in zip(t, nk)]
    for lvl in range(N_MERGE):
        et = [_dot(x * mask_ref[M_MERGE0 + lvl], y) for x, y in zip(a, t)]
        t = [x - _dot(x, y) for x, y in zip(t, et)]
    return t


def _gdn_kernel(qkv_ref, z_ref, sm_ref, cw_ref, arow_ref, acol_ref, drow_ref, dcol_ref, nw_ref, mask_ref, tri_ref,
                o_ref, carry_ref, state_ref):
    n = GDN_TILE
    heads = range(GDN_HEADS)

    @pl.when(pl.program_id(0) == 0)
    def _init():
        carry_ref[...] = jnp.zeros_like(carry_ref)
        state_ref[...] = jnp.zeros_like(state_ref)

    row8 = lax.broadcasted_iota(jnp.int32, (SUBLANES, LANES), 0)

    sm = sm_ref[...]
    g_cols = -jnp.exp(arow_ref[...]) * _softplus(sm + drow_ref[...])
    beta_cols = _sigmoid(sm)
    sm_t = sm.T
    ga = slice(SM_GA, SM_GA + GDN_HEADS)
    g_rows = -jnp.exp(acol_ref[ga, :]) * _softplus(sm_t[ga, :] + dcol_ref[ga, :])
    gc_cols = sum(jnp.dot(tri_ref[0], part, preferred_element_type=F32) for part in _split3(g_cols))
    gc_rows = sum(jnp.dot(part, tri_ref[1], preferred_element_type=F32) for part in _split3(g_rows))

    def conv_silu(off):
        x = qkv_ref[:, off:off + LANES]
        prev = carry_ref[:, off:off + LANES]
        w = cw_ref[:, off:off + LANES]
        acc = x * w[GDN_CONV - 1:GDN_CONV, :]
        for s in range(1, GDN_CONV):
            xs = pltpu.roll(x, s, axis=0)
            top = jnp.where(row8 < s, pltpu.roll(prev, s, axis=0), xs[0:SUBLANES, :])
            xs = jnp.concatenate([top, xs[SUBLANES:, :]], axis=0)
            acc = acc + xs * w[GDN_CONV - 1 - s:GDN_CONV - s, :]
        return acc * _sigmoid(acc)

    q = [conv_silu(h * HEAD_DIM) for h in heads]
    k = [conv_silu(GDN_WIDTH + h * HEAD_DIM) for h in heads]
    v = [conv_silu(2 * GDN_WIDTH + h * HEAD_DIM) for h in heads]
    q = [x * lax.rsqrt(jnp.sum(x * x, -1, keepdims=True) + RMS_EPS) * (HEAD_DIM ** -0.5) for x in q]
    k = [x * lax.rsqrt(jnp.sum(x * x, -1, keepdims=True) + RMS_EPS) for x in k]
    gcol = [gc_cols[:, SM_GA + h:SM_GA + h + 1] for h in heads]
    grow = [gc_rows[h:h + 1, :] for h in heads]
    bcol = [beta_cols[:, SM_GB + h:SM_GB + h + 1] for h in heads]
    glast = [x[n - 1:n, :] for x in gcol]
    ecol = [jnp.exp(x) for x in gcol]
    decay = [mask_ref[M_LOWER] * jnp.exp(jnp.minimum(c - r, 0.0)) for c, r in zip(gcol, grow)]
    k_t = [x.T for x in k]
    k_tb = [x.astype(BF) for x in k_t]
    kk = [jnp.dot(x.astype(BF), y, preferred_element_type=F32) for x, y in zip(k, k_tb)]
    qk = [jnp.dot(x.astype(BF), y, preferred_element_type=F32) for x, y in zip(q, k_tb)]
    a = [mask_ref[M_STRICT] * (b * x * d) for b, x, d in zip(bcol, kk, decay)]
    t = _unit_lower_inverses(a, mask_ref)
    rhs = [jnp.concatenate([v[h] * bcol[h], k[h] * (bcol[h] * ecol[h])], axis=1) for h in heads]
    sol = [_dot(x, y) for x, y in zip(t, rhs)]
    state = [state_ref[h] for h in heads]
    state_b = [x.astype(BF) for x in state]
    v_new = [x[:, :HEAD_DIM] - jnp.dot(x[:, HEAD_DIM:].astype(BF), s, preferred_element_type=F32)
             for x, s in zip(sol, state_b)]
    v_nb = [x.astype(BF) for x in v_new]
    o = [jnp.dot((q[h] * ecol[h]).astype(BF), state_b[h], preferred_element_type=F32)
         + jnp.dot((qk[h] * decay[h]).astype(BF), v_nb[h], preferred_element_type=F32) for h in heads]
    for h in heads:
        k_dec_t = k_t[h] * jnp.exp(glast[h] - grow[h])
        state_ref[h] = state[h] * jnp.exp(glast[h]) + jnp.dot(k_dec_t.astype(BF), v_nb[h],
                                                              preferred_element_type=F32)
    for h in heads:
        on = o[h] * lax.rsqrt(jnp.mean(o[h] * o[h], -1, keepdims=True) + RMS_EPS) * nw_ref[...]
        z = z_ref[:, h * HEAD_DIM:(h + 1) * HEAD_DIM]
        o_ref[:, h * HEAD_DIM:(h + 1) * HEAD_DIM] = (on * (z * _sigmoid(z))).astype(o_ref.dtype)

    carry_ref[...] = qkv_ref[n - SUBLANES:n, :]


def _gdn(pm, pt, conv_w, arow, acol, drow, dcol, norm_w, masks, tri, layer):
    s = pm.shape[0]
    n = GDN_TILE
    lsel3 = lambda i: (layer, 0, 0)
    const3 = lambda i: (0, 0, 0)
    return pl.pallas_call(
        _gdn_kernel,
        out_shape=jax.ShapeDtypeStruct((s, GDN_WIDTH), BF),
        grid=(s // n,),
        in_specs=[pl.BlockSpec((n, 3 * GDN_WIDTH), lambda i: (i, COL_GDN_QKV // (3 * GDN_WIDTH))),
                  pl.BlockSpec((n, GDN_WIDTH), lambda i: (i, COL_GDN_Z // GDN_WIDTH)),
                  pl.BlockSpec((n, LANES), lambda i: (i, COL_SMALL // LANES)),
                  pl.BlockSpec((None, GDN_CONV, 3 * GDN_WIDTH), lsel3),
                  pl.BlockSpec((None, 1, LANES), lsel3),
                  pl.BlockSpec((None, LANES, 1), lsel3),
                  pl.BlockSpec((None, 1, LANES), lsel3),
                  pl.BlockSpec((None, LANES, 1), lsel3),
                  pl.BlockSpec((None, 1, HEAD_DIM), lsel3),
                  pl.BlockSpec(masks.shape, const3, pipeline_mode=pl.Buffered(1)),
                  pl.BlockSpec(tri.shape, const3, pipeline_mode=pl.Buffered(1))],
        out_specs=pl.BlockSpec((n, GDN_WIDTH), lambda i: (i, 0)),
        scratch_shapes=[pltpu.VMEM((SUBLANES, 3 * GDN_WIDTH), F32),
                        pltpu.VMEM((GDN_HEADS, HEAD_DIM, HEAD_DIM), F32)],
        compiler_params=_cparams("arbitrary"),
        name="gdn",
    )(pm, pm, pt, conv_w, arow, acol, drow, dcol, norm_w, masks, tri)


def _dsa_prep_kernel(aq_ref, iq_ref, ckv_ref, sm_ref, wk_ref, wvt_ref, kvg_ref, ig_ref, ib_ref,
                     cos_ref, sins_ref, cos_t_ref, sin_t_ref, icos_t_ref, isin_t_ref, icos_ref, isin_ref,
                     qt_ref, k_ref, vt_ref, qit_ref, ki_ref, wt_ref):
    half = HEAD_DIM // 2
    ihalf = IDX_DIM // 2

    aq_t = aq_ref[...].T
    cos_t = cos_t_ref[...]
    sin_t = sin_t_ref[...]
    for h in range(ATT_HEADS):
        x = aq_t[h * HEAD_DIM:(h + 1) * HEAD_DIM, :]
        rot = jnp.concatenate([-x[half:, :], x[:half, :]], axis=0)
        qt_ref[h] = ((x * cos_t + rot * sin_t) * (LOG2E * HEAD_DIM ** -0.5)).astype(qt_ref.dtype)

    iq_t = iq_ref[...].T
    icos_t = icos_t_ref[...]
    isin_t = isin_t_ref[...]
    for h in range(IDX_HEADS):
        x = iq_t[h * IDX_DIM:(h + 1) * IDX_DIM, :]
        rot = jnp.concatenate([-x[ihalf:, :], x[:ihalf, :]], axis=0)
        qit_ref[h * IDX_DIM:(h + 1) * IDX_DIM, :] = (
            (x * icos_t + rot * isin_t) * (IDX_DIM ** -0.5)).astype(qit_ref.dtype)

    c = ckv_ref[...]
    kvn = c * lax.rsqrt(jnp.mean(c * c, -1, keepdims=True) + RMS_EPS) * kvg_ref[...]
    k = jnp.dot(kvn.astype(BF), wk_ref[...], preferred_element_type=F32)
    cos = cos_ref[...]
    sins = sins_ref[...]
    for h in range(ATT_HEADS):
        x = k[:, h * HEAD_DIM:(h + 1) * HEAD_DIM]
        k_ref[h] = (x * cos + pltpu.roll(x, half, axis=1) * sins).astype(k_ref.dtype)
    v_t = jnp.dot(wvt_ref[...], kvn.T.astype(BF), preferred_element_type=F32)
    for h in range(ATT_HEADS):
        vt_ref[h, 0, :HEAD_DIM, :] = v_t[h * HEAD_DIM:(h + 1) * HEAD_DIM, :].astype(vt_ref.dtype)
        vt_ref[h, 0, HEAD_DIM:, :] = jnp.ones((V_ROWS - HEAD_DIM, v_t.shape[1]), vt_ref.dtype)

    sm = sm_ref[...]
    lane = lax.broadcasted_iota(jnp.int32, sm.shape, 1)
    is_k = lane < IDX_DIM
    mu = jnp.sum(jnp.where(is_k, sm, 0.0), -1, keepdims=True) * (1.0 / IDX_DIM)
    d = jnp.where(is_k, sm - mu, 0.0)
    var = jnp.sum(d * d, -1, keepdims=True) * (1.0 / IDX_DIM)
    kin = d * lax.rsqrt(var + LN_EPS) * ig_ref[...] + ib_ref[...]
    below = pltpu.roll(kin, ihalf, axis=1)
    above = pltpu.roll(kin, LANES - ihalf, axis=1)
    rot = jnp.where((lane & (IDX_DIM - 1)) < ihalf, -above, below)
    kir = kin * icos_ref[...] + rot * isin_ref[...]
    ki_ref[...] = kir[:, :IDX_DIM].astype(ki_ref.dtype)
    wt_ref[...] = sm.T[SM_IW:SM_IW + IDX_HEADS, :] * (IDX_HEADS ** -0.5)


def _dsa_prep(p, wk, wvt, kvg, ig, ib, tabs, layer):
    s = p.shape[0]
    tp = KC
    lsel3 = lambda i: (layer, 0, 0)
    row = lambda i: (i, 0)
    col = lambda i: (0, i)
    out_shape = (jax.ShapeDtypeStruct((ATT_HEADS, HEAD_DIM, s), BF),
                 jax.ShapeDtypeStruct((ATT_HEADS, s, HEAD_DIM), BF),
                 jax.ShapeDtypeStruct((ATT_HEADS, s // tp, V_ROWS, tp), BF),
                 jax.ShapeDtypeStruct((IDX_HEADS * IDX_DIM, s), BF),
                 jax.ShapeDtypeStruct((s, IDX_DIM), BF),
                 jax.ShapeDtypeStruct((IDX_HEADS, s), F32))
    return pl.pallas_call(
        _dsa_prep_kernel,
        out_shape=out_shape,
        grid=(s // tp,),
        in_specs=[pl.BlockSpec((tp, ATT_WIDTH), lambda i: (i, COL_ATT_Q // ATT_WIDTH)),
                  pl.BlockSpec((tp, IDX_HEADS * IDX_DIM), lambda i: (i, COL_IDX_Q // (IDX_HEADS * IDX_DIM))),
                  pl.BlockSpec((tp, KV_RANK), lambda i: (i, COL_KV // KV_RANK)),
                  pl.BlockSpec((tp, LANES), lambda i: (i, COL_SMALL // LANES)),
                  pl.BlockSpec((None, KV_RANK, ATT_WIDTH), lsel3),
                  pl.BlockSpec((None, ATT_WIDTH, KV_RANK), lsel3),
                  pl.BlockSpec((None, 1, KV_RANK), lsel3),
                  pl.BlockSpec((None, 1, LANES), lsel3),
                  pl.BlockSpec((None, 1, LANES), lsel3),
                  pl.BlockSpec((tp, HEAD_DIM), row), pl.BlockSpec((tp, HEAD_DIM), row),
                  pl.BlockSpec((HEAD_DIM, tp), col), pl.BlockSpec((HEAD_DIM, tp), col),
                  pl.BlockSpec((IDX_DIM, tp), col), pl.BlockSpec((IDX_DIM, tp), col),
                  pl.BlockSpec((tp, LANES), row), pl.BlockSpec((tp, LANES), row)],
        out_specs=(pl.BlockSpec((ATT_HEADS, HEAD_DIM, tp), lambda i: (0, 0, i)),
                   pl.BlockSpec((ATT_HEADS, tp, HEAD_DIM), lambda i: (0, i, 0)),
                   pl.BlockSpec((ATT_HEADS, 1, V_ROWS, tp), lambda i: (0, i, 0, 0)),
                   pl.BlockSpec((IDX_HEADS * IDX_DIM, tp), col),
                   pl.BlockSpec((tp, IDX_DIM), row),
                   pl.BlockSpec((IDX_HEADS, tp), col)),
        compiler_params=_cparams("parallel"),
        name="dsa_prep",
    )(p, p, p, p, wk, wvt, kvg, ig, ib, *tabs)


def _dsa_kernel(qit_ref, wt_ref, ki_ref, qt_ref, k_ref, vt_ref, o_ref, sc_ref, sh_ref, m_ref, al_ref, acc_ref,
                s_ref, *, top_k):
    i = pl.program_id(0)
    n_chunks = i + 1
    rowi = lax.broadcasted_iota(jnp.int32, (KC, TQ), 0)
    qpos = i * TQ + lax.broadcasted_iota(jnp.int32, (KC, TQ), 1)

    def chunk(c):
        return pl.ds(pl.multiple_of(c * KC, KC), KC)

    w_t = wt_ref[...]

    def score_body(c, carry):
        kic = ki_ref[chunk(c), :]
        acc = jnp.zeros((KC, TQ), F32)
        for h in range(IDX_HEADS):
            d = jnp.dot(kic, qit_ref[h * IDX_DIM:(h + 1) * IDX_DIM, :], preferred_element_type=F32)
            acc = acc + w_t[h:h + 1, :] * jnp.maximum(d, 0.0)
        visible = (c * KC + rowi) <= qpos
        score = jnp.where(visible, acc, jnp.nan)
        sc_ref[chunk(c), :] = score
        sh_ref[chunk(c), :] = high_half(score).astype(sh_ref.dtype)
        return carry

    def high_half(v):
        return lax.bitcast_convert_type(lax.bitcast_convert_type(v, jnp.int32) & jnp.int32(-2 ** 16), F32)

    lax.fori_loop(0, n_chunks, score_body, 0)
    n_pairs = lax.shift_right_logical(n_chunks, 1)

    def pair(c2):
        return pl.ds(pl.multiple_of(c2 * (2 * KC), 2 * KC), 2 * KC)

    def count(pred):
        def part(rows):
            one = jnp.where(pred(sc_ref[rows, :]), 1.0, 0.0)
            return jnp.sum(one.reshape(-1, COUNT_ROWS, TQ), axis=0)

        cnt = lax.fori_loop(0, n_pairs, lambda c2, cnt: cnt + part(pair(c2)), jnp.zeros((COUNT_ROWS, TQ), F32))
        cnt = lax.fori_loop(2 * n_pairs, n_chunks, lambda c, cnt: cnt + part(chunk(c)), cnt)
        return jnp.sum(cnt, axis=0, keepdims=True)

    def count_high(thr_h):
        def part(rows):
            one = jnp.where(sh_ref[rows, :] >= thr_h, jnp.ones((), BF), jnp.zeros((), BF))
            one = one.reshape(-1, COUNT_ROWS, TQ)
            tot = one[0]
            for g in range(1, one.shape[0]):
                tot = tot + one[g]
            return tot

        cnt = lax.fori_loop(0, n_pairs, lambda c2, cnt: cnt + part(pair(c2)), jnp.zeros((COUNT_ROWS, TQ), BF))
        cnt = lax.fori_loop(2 * n_pairs, n_chunks, lambda c, cnt: cnt + part(chunk(c)), cnt)
        return jnp.sum(cnt.astype(F32), axis=0, keepdims=True)

    def key_to_f32(cu):
        ks = cu ^ jnp.int32(-2 ** 31)
        bits = jnp.where(ks < 0, ks ^ jnp.int32(2 ** 31 - 1), ks)
        return lax.bitcast_convert_type(bits, F32)

    def bit_pass(it, st, high=False):
        cu, done = st
        cand = cu | lax.shift_left(jnp.int32(1), 31 - it)
        thr_c = key_to_f32(cand)
        cnt = count_high(high_half(thr_c).astype(BF)) if high else count(lambda x: x >= thr_c)
        take = jnp.logical_and(cnt >= top_k, done == 0)
        return jnp.where(take, cand, cu), jnp.where(jnp.logical_and(take, cnt == top_k), 1, done)

    def bits_left(st):
        it, _, done = st
        return jnp.logical_and(it < 32, jnp.min(done) == 0)

    def two_passes(st):
        it, cu, done = st
        cu, done = lax.fori_loop(it, it + 2, bit_pass, (cu, done))
        return it + 2, cu, done

    st = lax.fori_loop(0, 16, functools.partial(bit_pass, high=True),
                       (jnp.zeros((1, TQ), jnp.int32), jnp.zeros((1, TQ), jnp.int32)))
    st = lax.fori_loop(16, UNTESTED_BITS, bit_pass, st)
    _, cu, _ = lax.while_loop(bits_left, two_passes, (jnp.int32(UNTESTED_BITS),) + st)
    thr = jnp.where((cu & jnp.int32(-2 ** 23)) == 0, -jnp.inf, key_to_f32(cu))
    n_ge = count(lambda x: x >= thr)
    has_tie = jnp.max(n_ge) > top_k

    @pl.when(jnp.logical_not(has_tie))
    def _select():
        def body(c, carry):
            sc_ref[chunk(c), :] = jnp.where(sc_ref[chunk(c), :] >= thr, 0.0, NEG_BIAS)
            return carry
        lax.fori_loop(0, n_chunks, body, 0)

    @pl.when(has_tie)
    def _select_ties():
        need = top_k - count(lambda x: x > thr)
        ltri = jnp.where(lax.broadcasted_iota(jnp.int32, (KC, KC), 1)
                         <= lax.broadcasted_iota(jnp.int32, (KC, KC), 0), 1.0, 0.0).astype(BF)

        def body(c, seen):
            x = sc_ref[chunk(c), :]
            eq = x == thr
            rank = jnp.dot(ltri, jnp.where(eq, 1.0, 0.0).astype(BF), preferred_element_type=F32) + seen
            keep = jnp.logical_or(x > thr, jnp.logical_and(eq, rank <= need))
            sc_ref[chunk(c), :] = jnp.where(keep, 0.0, NEG_BIAS)
            return rank[KC - 1:KC, :]
        lax.fori_loop(0, n_chunks, body, jnp.zeros((1, TQ), F32))

    m_ref[...] = jnp.full(m_ref.shape, -jnp.inf, F32)
    acc_ref[...] = jnp.zeros_like(acc_ref)

    def att_body(c, carry):
        bias = sc_ref[chunk(c), :]
        for h in range(ATT_HEADS):
            s = jnp.dot(k_ref[h, chunk(c), :], qt_ref[h], preferred_element_type=F32) + bias
            s_ref[h] = s
            m_old = m_ref[h:h + 1, :]
            m_new = jnp.maximum(m_old, jnp.max(s, axis=0, keepdims=True))
            al_ref[h:h + 1, :] = jnp.exp2(m_old - m_new)
            m_ref[h:h + 1, :] = m_new
        for h in range(ATT_HEADS):
            p = jnp.exp2(s_ref[h] - m_ref[h:h + 1, :])
            acc_ref[h] = al_ref[h:h + 1, :] * acc_ref[h] + jnp.dot(
                vt_ref[h, c], p.astype(BF), preferred_element_type=F32)
        return carry

    lax.fori_loop(0, n_chunks, att_body, 0)
    for h in range(ATT_HEADS):
        o_t = acc_ref[h, :HEAD_DIM, :] / acc_ref[h, HEAD_DIM:HEAD_DIM + 1, :]
        o_ref[:, h * HEAD_DIM:(h + 1) * HEAD_DIM] = o_t.T.astype(o_ref.dtype)


def _dsa(qt, k3, vt4, qit, ki, wt):
    s = ki.shape[0]
    top_k = min(INDEX_TOPK, s // 4)
    resident = pl.Buffered(1)
    return pl.pallas_call(
        functools.partial(_dsa_kernel, top_k=top_k),
        out_shape=jax.ShapeDtypeStruct((s, ATT_WIDTH), BF),
        grid=(s // TQ,),
        in_specs=[pl.BlockSpec((IDX_HEADS * IDX_DIM, TQ), lambda i: (0, i)),
                  pl.BlockSpec((IDX_HEADS, TQ), lambda i: (0, i)),
                  pl.BlockSpec((s, IDX_DIM), lambda i: (0, 0), pipeline_mode=resident),
                  pl.BlockSpec((ATT_HEADS, HEAD_DIM, TQ), lambda i: (0, 0, i)),
                  pl.BlockSpec((ATT_HEADS, s, HEAD_DIM), lambda i: (0, 0, 0), pipeline_mode=resident),
                  pl.BlockSpec((ATT_HEADS, s // KC, V_ROWS, KC), lambda i: (0, 0, 0, 0),
                               pipeline_mode=resident)],
        out_specs=pl.BlockSpec((TQ, ATT_WIDTH), lambda i: (i, 0)),
        scratch_shapes=[pltpu.VMEM((s, TQ), F32),
                        pltpu.VMEM((s, TQ), BF),
                        pltpu.VMEM((ATT_HEADS, TQ), F32),
                        pltpu.VMEM((ATT_HEADS, TQ), F32),
                        pltpu.VMEM((ATT_HEADS, V_ROWS, TQ), F32),
                        pltpu.VMEM((ATT_HEADS, KC, TQ), F32)],
        compiler_params=_cparams("arbitrary"),
        name="dsa",
    )(qit, wt, ki, qt, k3, vt4)


def _layer_norm(r, g, b):
    mu = jnp.mean(r, -1, keepdims=True)
    d = r - mu
    var = jnp.mean(d * d, -1, keepdims=True)
    return d * lax.rsqrt(var + LN_EPS) * g + b


def _out_ln_kernel(a1_ref, a2_ref, w1_ref, w2_ref, x_ref, g_ref, b_ref, xo_ref, xb_ref):
    y = (jnp.dot(a1_ref[...], w1_ref[...], preferred_element_type=F32)
         + jnp.dot(a2_ref[...], w2_ref[...], preferred_element_type=F32))
    o = _layer_norm(DN_ALPHA * x_ref[...] + y, g_ref[...], b_ref[...])
    xo_ref[...] = o
    xb_ref[...] = o.astype(xb_ref.dtype)


def _out_ln(o_gdn, o_dsa, w_out, xf, g, b, layer):
    s = xf.shape[0]
    tm = min(512, s)
    row = lambda i: (i, 0)
    lsel3 = lambda i: (layer, 0, 0)
    return pl.pallas_call(
        _out_ln_kernel,
        out_shape=(jax.ShapeDtypeStruct((s, D_MODEL), F32), jax.ShapeDtypeStruct((s, D_MODEL), BF)),
        grid=(s // tm,),
        in_specs=[pl.BlockSpec((tm, GDN_WIDTH), row), pl.BlockSpec((tm, ATT_WIDTH), row),
                  pl.BlockSpec((None, GDN_WIDTH, D_MODEL), lsel3, pipeline_mode=pl.Buffered(1)),
                  pl.BlockSpec((None, ATT_WIDTH, D_MODEL), lambda i: (layer, 1, 0),
                               pipeline_mode=pl.Buffered(1)),
                  pl.BlockSpec((tm, D_MODEL), row),
                  pl.BlockSpec((None, 1, D_MODEL), lsel3), pl.BlockSpec((None, 1, D_MODEL), lsel3)],
        out_specs=(pl.BlockSpec((tm, D_MODEL), row), pl.BlockSpec((tm, D_MODEL), row)),
        compiler_params=_cparams("parallel"),
        name="out_ln",
    )(o_gdn, o_dsa, w_out, w_out, xf, g, b)


def _ffn_up_kernel(x_ref, wg_ref, wv_ref, cg_ref, cv_ref, bg_ref, bv_ref, h_ref, carry_ref, wb_ref):
    tm = x_ref.shape[0]

    @pl.when(pl.program_id(1) == 0)
    def _init():
        carry_ref[...] = jnp.zeros_like(carry_ref)
        wb_ref[0] = wg_ref[...].astype(wb_ref.dtype)
        wb_ref[1] = wv_ref[...].astype(wb_ref.dtype)

    row8 = lax.broadcasted_iota(jnp.int32, (SUBLANES, h_ref.shape[1]), 0)

    def conv(u, prev, w, b):
        acc = u * w[FFN_CONV - 1:FFN_CONV, :] + b
        for s in range(1, FFN_CONV):
            us = pltpu.roll(u, s, axis=0)
            top = jnp.where(row8 < s, pltpu.roll(prev, s, axis=0), us[0:SUBLANES, :])
            us = jnp.concatenate([top, us[SUBLANES:, :]], axis=0)
            acc = acc + us * w[FFN_CONV - 1 - s:FFN_CONV - s, :]
        return acc

    x = x_ref[...]
    ug = jnp.dot(x, wb_ref[0], preferred_element_type=F32)
    uv = jnp.dot(x, wb_ref[1], preferred_element_type=F32)
    gate = conv(ug, carry_ref[0], cg_ref[...], bg_ref[...])
    val = conv(uv, carry_ref[1], cv_ref[...], bv_ref[...])
    carry_ref[0] = ug[tm - SUBLANES:, :]
    carry_ref[1] = uv[tm - SUBLANES:, :]
    h_ref[...] = (gate * _sigmoid(gate) * val).astype(h_ref.dtype)


def _ffn_up(xb, w_up, conv_w, conv_b, layer):
    s = xb.shape[0]
    tm = min(512, s)
    tn = 512
    nj = D_FF // tn
    return pl.pallas_call(
        _ffn_up_kernel,
        out_shape=jax.ShapeDtypeStruct((s, D_FF), BF),
        grid=(nj, s // tm),
        in_specs=[pl.BlockSpec((tm, D_MODEL), lambda j, i: (i, 0)),
                  pl.BlockSpec((None, D_MODEL, tn), lambda j, i: (layer, 0, j)),
                  pl.BlockSpec((None, D_MODEL, tn), lambda j, i: (layer, 0, j + nj)),
                  pl.BlockSpec((None, FFN_CONV, tn), lambda j, i: (layer, 0, j)),
                  pl.BlockSpec((None, FFN_CONV, tn), lambda j, i: (layer, 0, j + nj)),
                  pl.BlockSpec((None, 1, tn), lambda j, i: (layer, 0, j)),
                  pl.BlockSpec((None, 1, tn), lambda j, i: (layer, 0, j + nj))],
        out_specs=pl.BlockSpec((tm, tn), lambda j, i: (i, j)),
        scratch_shapes=[pltpu.VMEM((2, SUBLANES, tn), F32),
                        pltpu.VMEM((2, D_MODEL, tn), BF)],
        compiler_params=_cparams("arbitrary", "arbitrary"),
        name="ffn_up",
    )(xb, w_up, w_up, conv_w, conv_w, conv_b, conv_b)


def _ffn_down_kernel(h_ref, w_ref, x_ref, g_ref, b_ref, xo_ref, xb_ref):
    f = jnp.dot(h_ref[...], w_ref[...], preferred_element_type=F32)
    o = _layer_norm(DN_ALPHA * x_ref[...] + f, g_ref[...], b_ref[...])
    xo_ref[...] = o
    xb_ref[...] = o.astype(xb_ref.dtype)


def _ffn_down_ln(h, w_down, xf, g, b, layer):
    s = xf.shape[0]
    tm = min(256, s)
    row = lambda i: (i, 0)
    lsel3 = lambda i: (layer, 0, 0)
    return pl.pallas_call(
        _ffn_down_kernel,
        out_shape=(jax.ShapeDtypeStruct((s, D_MODEL), F32), jax.ShapeDtypeStruct((s, D_MODEL), BF)),
        grid=(s // tm,),
        in_specs=[pl.BlockSpec((tm, D_FF), row),
                  pl.BlockSpec((None, D_FF, D_MODEL), lsel3, pipeline_mode=pl.Buffered(1)),
                  pl.BlockSpec((tm, D_MODEL), row),
                  pl.BlockSpec((None, 1, D_MODEL), lsel3), pl.BlockSpec((None, 1, D_MODEL), lsel3)],
        out_specs=(pl.BlockSpec((tm, D_MODEL), row), pl.BlockSpec((tm, D_MODEL), row)),
        compiler_params=_cparams("parallel"),
        name="ffn_down",
    )(h, w_down, xf, g, b)


def _rope_tables(seq, dim):
    inv = ROPE_THETA ** (-jnp.arange(0, dim, 2, dtype=F32) / dim)
    ang = jnp.arange(seq, dtype=F32)[:, None] * inv[None, :]
    ang = jnp.concatenate([ang, ang], -1)
    return jnp.cos(ang), jnp.sin(ang)


def _lane_pad(v, offset):
    out = jnp.zeros((v.shape[0], 1, LANES), F32)
    return out.at[:, 0, offset:offset + v.shape[1]].set(v.astype(F32))


def kernel(x, w_in, gdn_conv_w, gdn_a_log, gdn_dt_bias, gdn_norm_w, kv_norm_w, w_ukv, idx_k_norm_g,
           idx_k_norm_b, w_out, ln1_g, ln1_b, ffn_up, ffn_conv_w, ffn_conv_b, ffn_down, ln2_g, ln2_b):
    seq = x.shape[1]
    depth = w_in.shape[0]

    o_ga = MAIN_WIDTH
    o_aq = o_ga + 2 * GDN_HEADS
    o_ckv = o_aq + ATT_WIDTH
    o_iq = o_ckv + KV_RANK
    o_ik = o_iq + IDX_HEADS * IDX_DIM
    o_iw = o_ik + IDX_DIM
    o_end = o_iw + IDX_HEADS
    w_tail = jnp.concatenate(
        [w_in[:, :, o_aq:o_ckv], w_in[:, :, o_iq:o_ik], w_in[:, :, o_ckv:o_iq], w_in[:, :, o_ik:o_end],
         w_in[:, :, o_ga:o_aq], jnp.zeros((depth, D_MODEL, TAIL_WIDTH - (o_end - o_ga)), w_in.dtype)],
        axis=-1).astype(BF)
    wk = w_ukv[:, :, :ATT_WIDTH].astype(BF)
    wvt = jnp.swapaxes(w_ukv[:, :, ATT_WIDTH:], 1, 2).astype(BF)
    w_out_b = w_out.astype(BF)
    w_down_b = ffn_down.astype(BF)

    arow = _lane_pad(gdn_a_log, SM_GA)
    drow = _lane_pad(gdn_dt_bias, SM_GA)
    acol = jnp.swapaxes(arow, 1, 2)
    dcol = jnp.swapaxes(drow, 1, 2)
    ig = _lane_pad(idx_k_norm_g, 0)
    ib = _lane_pad(idx_k_norm_b, 0)
    r3 = lambda v: v.astype(F32)[:, None, :]

    cos, sin = _rope_tables(seq, HEAD_DIM)
    icos, isin = _rope_tables(seq, IDX_DIM)
    sign = jnp.where(jnp.arange(HEAD_DIM) < HEAD_DIM // 2, -1.0, 1.0).astype(F32)
    lane_fill = lambda t: jnp.concatenate([t, jnp.zeros_like(t)], axis=-1)
    tabs = (cos, sin * sign, cos.T, sin.T, icos.T, isin.T, lane_fill(icos), lane_fill(isin))

    masks, tri = _gdn_masks()

    xf = x[0]
    xb = xf.astype(BF)
    for layer in range(depth):
        pm = _proj_main(xb, w_in, layer)
        pt = _proj_tail(xb, w_tail, layer)
        o_gdn = _gdn(pm, pt, gdn_conv_w, arow, acol, drow, dcol, r3(gdn_norm_w), masks, tri, layer)
        qt, k3, vt4, qit, ki, wt = _dsa_prep(pt, wk, wvt, r3(kv_norm_w), ig, ib, tabs, layer)
        o_dsa = _dsa(qt, k3, vt4, qit, ki, wt)
        xf, xb = _out_ln(o_gdn, o_dsa, w_out_b, xf, r3(ln1_g), r3(ln1_b), layer)
        h = _ffn_up(xb, ffn_up, ffn_conv_w, r3(ffn_conv_b), layer)
        xf, xb = _ffn_down_ln(h, w_down_b, xf, r3(ln2_g), r3(ln2_b), layer)
    return xf[None]
```

```python
import functools

import jax
import jax.numpy as jnp
import numpy as np
from jax import lax
from jax.experimental import pallas as pl
from jax.experimental.pallas import tpu as pltpu

D_MODEL = 2048
DEPTH = 4
HEAD_DIM = 128
GDN_HEADS = 8
ATT_HEADS = 8
GDN_WIDTH = GDN_HEADS * HEAD_DIM
ATT_WIDTH = ATT_HEADS * HEAD_DIM
GDN_CONV = 4
KV_RANK = 256
IDX_HEADS = 16
IDX_DIM = 64
INDEX_TOPK = 256
ROPE_THETA = 10000.0
D_FF = 5632
FFN_CONV = 3
LN_EPS = 1e-5
RMS_EPS = 1e-6
DN_ALPHA = (2 * DEPTH) ** 0.25

BF = jnp.bfloat16
F32 = jnp.float32
HIGHEST = lax.Precision.HIGHEST

LANES = 128
SUBLANES = 8
VMEM_LIMIT_BYTES = 56 * 1024 * 1024

MAIN_WIDTH = 4 * GDN_WIDTH
COL_GDN_QKV = 0
COL_GDN_Z = 3 * GDN_WIDTH
COL_ATT_Q = 0
COL_IDX_Q = COL_ATT_Q + ATT_WIDTH
COL_KV = COL_IDX_Q + IDX_HEADS * IDX_DIM
COL_SMALL = COL_KV + KV_RANK
SM_IW = IDX_DIM
SM_GA = SM_IW + IDX_HEADS
SM_GB = SM_GA + GDN_HEADS
TAIL_WIDTH = COL_SMALL + 2 * LANES

GDN_TILE = 256
INV_BASE = 16
TQ = 256
KC = 256
COUNT_ROWS = 64
UNTESTED_BITS = 22
NEG_BIAS = -1e30
V_ROWS = HEAD_DIM + 16
LOG2E = 1.4426950408889634


def _cparams(*sem):
    return pltpu.CompilerParams(dimension_semantics=sem, vmem_limit_bytes=VMEM_LIMIT_BYTES)


def _sigmoid(x):
    return 1.0 / (1.0 + jnp.exp(-x))


def _softplus(x):
    return jnp.maximum(x, 0.0) + jnp.log(1.0 + jnp.exp(-jnp.abs(x)))


def _dot(a, b):
    return jnp.dot(a.astype(BF), b.astype(BF), preferred_element_type=F32)


def _mm_kernel(a_ref, b_ref, o_ref):
    o_ref[...] = jnp.dot(a_ref[...], b_ref[...], preferred_element_type=F32)


def _proj_main(xb, w_main, layer):
    s, k = xb.shape
    tm = min(1024, s)
    tn = 1024
    return pl.pallas_call(
        _mm_kernel,
        out_shape=jax.ShapeDtypeStruct((s, MAIN_WIDTH), F32),
        grid=(MAIN_WIDTH // tn, s // tm),
        in_specs=[pl.BlockSpec((tm, k), lambda j, i: (i, 0)),
                  pl.BlockSpec((None, k, tn), lambda j, i: (layer, 0, j))],
        out_specs=pl.BlockSpec((tm, tn), lambda j, i: (i, j)),
        compiler_params=_cparams("parallel", "parallel"),
        name="proj_main",
    )(xb, w_main)


def _proj_tail(xb, w_tail, layer):
    s, k = xb.shape
    tm = min(1024, s)
    tn = TAIL_WIDTH // 2
    return pl.pallas_call(
        _mm_kernel,
        out_shape=jax.ShapeDtypeStruct((s, TAIL_WIDTH), F32),
        grid=(TAIL_WIDTH // tn, s // tm),
        in_specs=[pl.BlockSpec((tm, k), lambda j, i: (i, 0)),
                  pl.BlockSpec((None, k, tn), lambda j, i: (layer, 0, j))],
        out_specs=pl.BlockSpec((tm, tn), lambda j, i: (i, j)),
        compiler_params=_cparams("parallel", "parallel"),
        name="proj_tail",
    )(xb, w_tail)


M_LOWER, M_STRICT, M_EYE, M_BASE, M_MERGE0 = 0, 1, 2, 3, 4
N_MERGE = int(np.log2(GDN_TILE // INV_BASE))


def _gdn_masks():
    r = np.arange(GDN_TILE)[:, None]
    c = np.arange(GDN_TILE)[None, :]
    same = lambda size: (r // size) == (c // size)
    rows = [c <= r, c < r, c == r, same(INV_BASE)]
    rows += [same(2 * INV_BASE << i) & ~same(INV_BASE << i) for i in range(N_MERGE)]
    masks = jnp.asarray(np.stack(rows).astype(np.float32))
    tri = jnp.asarray(np.stack([c <= r, r <= c]).astype(np.float32)).astype(BF)
    return masks, tri


def _split3(x):
    h1 = x.astype(BF)
    r1 = x - h1.astype(F32)
    h2 = r1.astype(BF)
    return h1, h2, (r1 - h2.astype(F32)).astype(BF)


def _unit_lower_inverses(a, mask_ref):
    nk = [-(x * mask_ref[M_BASE]) for x in a]
    t = [mask_ref[M_EYE] + x for x in nk]
    for _ in range(int(np.log2(INV_BASE)) - 1):
        nk = [_dot(x, x) for x in nk]
        t = [x + _dot(x, y) for x, y in zip(t, nk)]
    for lvl in range(N_MERGE):
        et = [_dot(x * mask_ref[M_MERGE0 + lvl], y) for x, y in zip(a, t)]
        t = [x - _dot(x, y) for x, y in zip(t, et)]
    return t


def _gdn_kernel(qkv_ref, z_ref, sm_ref, cw_ref, arow_ref, acol_ref, drow_ref, dcol_ref, nw_ref, mask_ref, tri_ref,
                o_ref, carry_ref, state_ref):
    n = GDN_TILE
    heads = range(GDN_HEADS)

    @pl.when(pl.program_id(0) == 0)
    def _init():
        carry_ref[...] = jnp.zeros_like(carry_ref)
        state_ref[...] = jnp.zeros_like(state_ref)

    row8 = lax.broadcasted_iota(jnp.int32, (SUBLANES, LANES), 0)

    sm = sm_ref[...]
    g_cols = -jnp.exp(arow_ref[...]) * _softplus(sm + drow_ref[...])
    beta_cols = _sigmoid(sm)
    sm_t = sm.T
    ga = slice(SM_GA, SM_GA + GDN_HEADS)
    g_rows = -jnp.exp(acol_ref[ga, :]) * _softplus(sm_t[ga, :] + dcol_ref[ga, :])
    gc_cols = sum(jnp.dot(tri_ref[0], part, preferred_element_type=F32) for part in _split3(g_cols))
    gc_rows = sum(jnp.dot(part, tri_ref[1], preferred_element_type=F32) for part in _split3(g_rows))

    def conv_silu(off):
        x = qkv_ref[:, off:off + LANES]
        prev = carry_ref[:, off:off + LANES]
        w = cw_ref[:, off:off + LANES]
        acc = x * w[GDN_CONV - 1:GDN_CONV, :]
        for s in range(1, GDN_CONV):
            xs = pltpu.roll(x, s, axis=0)
            top = jnp.where(row8 < s, pltpu.roll(prev, s, axis=0), xs[0:SUBLANES, :])
            xs = jnp.concatenate([top, xs[SUBLANES:, :]], axis=0)
            acc = acc + xs * w[GDN_CONV - 1 - s:GDN_CONV - s, :]
        return acc * _sigmoid(acc)

    q = [conv_silu(h * HEAD_DIM) for h in heads]
    k = [conv_silu(GDN_WIDTH + h * HEAD_DIM) for h in heads]
    v = [conv_silu(2 * GDN_WIDTH + h * HEAD_DIM) for h in heads]
    q = [x * lax.rsqrt(jnp.sum(x * x, -1, keepdims=True) + RMS_EPS) * (HEAD_DIM ** -0.5) for x in q]
    k = [x * lax.rsqrt(jnp.sum(x * x, -1, keepdims=True) + RMS_EPS) for x in k]
    gcol = [gc_cols[:, SM_GA + h:SM_GA + h + 1] for h in heads]
    grow = [gc_rows[h:h + 1, :] for h in heads]
    bcol = [beta_cols[:, SM_GB + h:SM_GB + h + 1] for h in heads]
    glast = [x[n - 1:n, :] for x in gcol]
    ecol = [jnp.exp(x) for x in gcol]
    decay = [mask_ref[M_LOWER] * jnp.exp(jnp.minimum(c - r, 0.0)) for c, r in zip(gcol, grow)]
    k_t = [x.T for x in k]
    k_tb = [x.astype(BF) for x in k_t]
    kk = [jnp.dot(x.astype(BF), y, preferred_element_type=F32) for x, y in zip(k, k_tb)]
    qk = [jnp.dot(x.astype(BF), y, preferred_element_type=F32) for x, y in zip(q, k_tb)]
    a = [mask_ref[M_STRICT] * (b * x * d) for b, x, d in zip(bcol, kk, decay)]
    t = _unit_lower_inverses(a, mask_ref)
    rhs = [jnp.concatenate([v[h] * bcol[h], k[h] * (bcol[h] * ecol[h])], axis=1) for h in heads]
    sol = [_dot(x, y) for x, y in zip(t, rhs)]
    state = [state_ref[h] for h in heads]
    state_b = [x.astype(BF) for x in state]
    v_new = [x[:, :HEAD_DIM] - jnp.dot(x[:, HEAD_DIM:].astype(BF), s, preferred_element_type=F32)
             for x, s in zip(sol, state_b)]
    v_nb = [x.astype(BF) for x in v_new]
    o = [jnp.dot((q[h] * ecol[h]).astype(BF), state_b[h], preferred_element_type=F32)
         + jnp.dot((qk[h] * decay[h]).astype(BF), v_nb[h], preferred_element_type=F32) for h in heads]
    for h in heads:
        k_dec_t = k_t[h] * jnp.exp(glast[h] - grow[h])
        state_ref[h] = state[h] * jnp.exp(glast[h]) + jnp.dot(k_dec_t.astype(BF), v_nb[h],
                                                              preferred_element_type=F32)
    for h in heads:
        on = o[h] * lax.rsqrt(jnp.mean(o[h] * o[h], -1, keepdims=True) + RMS_EPS) * nw_ref[...]
        z = z_ref[:, h * HEAD_DIM:(h + 1) * HEAD_DIM]
        o_ref[:, h * HEAD_DIM:(h + 1) * HEAD_DIM] = (on * (z * _sigmoid(z))).astype(o_ref.dtype)

    carry_ref[...] = qkv_ref[n - SUBLANES:n, :]


def _gdn(pm, pt, conv_w, arow, acol, drow, dcol, norm_w, masks, tri, layer):
    s = pm.shape[0]
    n = GDN_TILE
    lsel3 = lambda i: (layer, 0, 0)
    const3 = lambda i: (0, 0, 0)
    return pl.pallas_call(
        _gdn_kernel,
        out_shape=jax.ShapeDtypeStruct((s, GDN_WIDTH), BF),
        grid=(s // n,),
        in_specs=[pl.BlockSpec((n, 3 * GDN_WIDTH), lambda i: (i, COL_GDN_QKV // (3 * GDN_WIDTH))),
                  pl.BlockSpec((n, GDN_WIDTH), lambda i: (i, COL_GDN_Z // GDN_WIDTH)),
                  pl.BlockSpec((n, LANES), lambda i: (i, COL_SMALL // LANES)),
                  pl.BlockSpec((None, GDN_CONV, 3 * GDN_WIDTH), lsel3),
                  pl.BlockSpec((None, 1, LANES), lsel3),
                  pl.BlockSpec((None, LANES, 1), lsel3),
                  pl.BlockSpec((None, 1, LANES), lsel3),
                  pl.BlockSpec((None, LANES, 1), lsel3),
                  pl.BlockSpec((None, 1, HEAD_DIM), lsel3),
                  pl.BlockSpec(masks.shape, const3, pipeline_mode=pl.Buffered(1)),
                  pl.BlockSpec(tri.shape, const3, pipeline_mode=pl.Buffered(1))],
        out_specs=pl.BlockSpec((n, GDN_WIDTH), lambda i: (i, 0)),
        scratch_shapes=[pltpu.VMEM((SUBLANES, 3 * GDN_WIDTH), F32),
                        pltpu.VMEM((GDN_HEADS, HEAD_DIM, HEAD_DIM), F32)],
        compiler_params=_cparams("arbitrary"),
        name="gdn",
    )(pm, pm, pt, conv_w, arow, acol, drow, dcol, norm_w, masks, tri)


def _dsa_prep_kernel(aq_ref, iq_ref, ckv_ref, sm_ref, wk_ref, wvt_ref, kvg_ref, ig_ref, ib_ref,
                     cos_ref, sins_ref, cos_t_ref, sin_t_ref, icos_t_ref, isin_t_ref, icos_ref, isin_ref,
                     qt_ref, k_ref, vt_ref, qit_ref, ki_ref, wt_ref):
    half = HEAD_DIM // 2
    ihalf = IDX_DIM // 2

    aq_t = aq_ref[...].T
    cos_t = cos_t_ref[...]
    sin_t = sin_t_ref[...]
    for h in range(ATT_HEADS):
        x = aq_t[h * HEAD_DIM:(h + 1) * HEAD_DIM, :]
        rot = jnp.concatenate([-x[half:, :], x[:half, :]], axis=0)
        qt_ref[h] = ((x * cos_t + rot * sin_t) * (LOG2E * HEAD_DIM ** -0.5)).astype(qt_ref.dtype)

    iq_t = iq_ref[...].T
    icos_t = icos_t_ref[...]
    isin_t = isin_t_ref[...]
    for h in range(IDX_HEADS):
        x = iq_t[h * IDX_DIM:(h + 1) * IDX_DIM, :]
        rot = jnp.concatenate([-x[ihalf:, :], x[:ihalf, :]], axis=0)
        qit_ref[h * IDX_DIM:(h + 1) * IDX_DIM, :] = (
            (x * icos_t + rot * isin_t) * (IDX_DIM ** -0.5)).astype(qit_ref.dtype)

    c = ckv_ref[...]
    kvn = c * lax.rsqrt(jnp.mean(c * c, -1, keepdims=True) + RMS_EPS) * kvg_ref[...]
    k = jnp.dot(kvn.astype(BF), wk_ref[...], preferred_element_type=F32)
    cos = cos_ref[...]
    sins = sins_ref[...]
    for h in range(ATT_HEADS):
        x = k[:, h * HEAD_DIM:(h + 1) * HEAD_DIM]
        k_ref[h] = (x * cos + pltpu.roll(x, half, axis=1) * sins).astype(k_ref.dtype)
    v_t = jnp.dot(wvt_ref[...], kvn.T.astype(BF), preferred_element_type=F32)
    for h in range(ATT_HEADS):
        vt_ref[h, 0, :HEAD_DIM, :] = v_t[h * HEAD_DIM:(h + 1) * HEAD_DIM, :].astype(vt_ref.dtype)
        vt_ref[h, 0, HEAD_DIM:, :] = jnp.ones((V_ROWS - HEAD_DIM, v_t.shape[1]), vt_ref.dtype)

    sm = sm_ref[...]
    lane = lax.broadcasted_iota(jnp.int32, sm.shape, 1)
    is_k = lane < IDX_DIM
    mu = jnp.sum(jnp.where(is_k, sm, 0.0), -1, keepdims=True) * (1.0 / IDX_DIM)
    d = jnp.where(is_k, sm - mu, 0.0)
    var = jnp.sum(d * d, -1, keepdims=True) * (1.0 / IDX_DIM)
    kin = d * lax.rsqrt(var + LN_EPS) * ig_ref[...] + ib_ref[...]
    below = pltpu.roll(kin, ihalf, axis=1)
    above = pltpu.roll(kin, LANES - ihalf, axis=1)
    rot = jnp.where((lane & (IDX_DIM - 1)) < ihalf, -above, below)
    kir = kin * icos_ref[...] + rot * isin_ref[...]
    ki_ref[...] = kir[:, :IDX_DIM].astype(ki_ref.dtype)
    wt_ref[...] = sm.T[SM_IW:SM_IW + IDX_HEADS, :] * (IDX_HEADS ** -0.5)


def _dsa_prep(p, wk, wvt, kvg, ig, ib, tabs, layer):
    s = p.shape[0]
    tp = KC
    lsel3 = lambda i: (layer, 0, 0)
    row = lambda i: (i, 0)
    col = lambda i: (0, i)
    out_shape = (jax.ShapeDtypeStruct((ATT_HEADS, HEAD_DIM, s), BF),
                 jax.ShapeDtypeStruct((ATT_HEADS, s, HEAD_DIM), BF),
                 jax.ShapeDtypeStruct((ATT_HEADS, s // tp, V_ROWS, tp), BF),
                 jax.ShapeDtypeStruct((IDX_HEADS * IDX_DIM, s), BF),
                 jax.ShapeDtypeStruct((s, IDX_DIM), BF),
                 jax.ShapeDtypeStruct((IDX_HEADS, s), F32))
    return pl.pallas_call(
        _dsa_prep_kernel,
        out_shape=out_shape,
        grid=(s // tp,),
        in_specs=[pl.BlockSpec((tp, ATT_WIDTH), lambda i: (i, COL_ATT_Q // ATT_WIDTH)),
                  pl.BlockSpec((tp, IDX_HEADS * IDX_DIM), lambda i: (i, COL_IDX_Q // (IDX_HEADS * IDX_DIM))),
                  pl.BlockSpec((tp, KV_RANK), lambda i: (i, COL_KV // KV_RANK)),
                  pl.BlockSpec((tp, LANES), lambda i: (i, COL_SMALL // LANES)),
                  pl.BlockSpec((None, KV_RANK, ATT_WIDTH), lsel3),
                  pl.BlockSpec((None, ATT_WIDTH, KV_RANK), lsel3),
                  pl.BlockSpec((None, 1, KV_RANK), lsel3),
                  pl.BlockSpec((None, 1, LANES), lsel3),
                  pl.BlockSpec((None, 1, LANES), lsel3),
                  pl.BlockSpec((tp, HEAD_DIM), row), pl.BlockSpec((tp, HEAD_DIM), row),
                  pl.BlockSpec((HEAD_DIM, tp), col), pl.BlockSpec((HEAD_DIM, tp), col),
                  pl.BlockSpec((IDX_DIM, tp), col), pl.BlockSpec((IDX_DIM, tp), col),
                  pl.BlockSpec((tp, LANES), row), pl.BlockSpec((tp, LANES), row)],
        out_specs=(pl.BlockSpec((ATT_HEADS, HEAD_DIM, tp), lambda i: (0, 0, i)),
                   pl.BlockSpec((ATT_HEADS, tp, HEAD_DIM), lambda i: (0, i, 0)),
                   pl.BlockSpec((ATT_HEADS, 1, V_ROWS, tp), lambda i: (0, i, 0, 0)),
                   pl.BlockSpec((IDX_HEADS * IDX_DIM, tp), col),
                   pl.BlockSpec((tp, IDX_DIM), row),
                   pl.BlockSpec((IDX_HEADS, tp), col)),
        compiler_params=_cparams("parallel"),
        name="dsa_prep",
    )(p, p, p, p, wk, wvt, kvg, ig, ib, *tabs)


def _dsa_kernel(qit_ref, wt_ref, ki_ref, qt_ref, k_ref, vt_ref, o_ref, sc_ref, sh_ref, m_ref, al_ref, acc_ref,
                s_ref, *, top_k):
    i = pl.program_id(0)
    n_chunks = i + 1
    n_pairs = lax.shift_right_logical(n_chunks, 1)

    def chunk(c):
        return pl.ds(pl.multiple_of(c * KC, KC), KC)

    def pair(c2):
        return pl.ds(pl.multiple_of(c2 * (2 * KC), 2 * KC), 2 * KC)

    def high_half(v):
        return lax.bitcast_convert_type(lax.bitcast_convert_type(v, jnp.int32) & jnp.int32(-2 ** 16), F32)

    w_t = wt_ref[...]

    def score_rows(rows, n_rows, first_key):
        kic = ki_ref[rows, :]
        acc = jnp.zeros((n_rows, TQ), F32)
        for h in range(IDX_HEADS):
            d = jnp.dot(kic, qit_ref[h * IDX_DIM:(h + 1) * IDX_DIM, :], preferred_element_type=F32)
            acc = acc + w_t[h:h + 1, :] * jnp.maximum(d, 0.0)
        visible = (first_key + lax.broadcasted_iota(jnp.int32, (n_rows, TQ), 0)
                   <= i * TQ + lax.broadcasted_iota(jnp.int32, (n_rows, TQ), 1))
        score = jnp.where(visible, acc, jnp.nan)
        sc_ref[rows, :] = score
        sh_ref[rows, :] = high_half(score).astype(sh_ref.dtype)

    @pl.loop(0, n_pairs)
    def _score_pairs(c2):
        score_rows(pair(c2), 2 * KC, c2 * (2 * KC))

    @pl.loop(2 * n_pairs, n_chunks)
    def _score_rest(c):
        score_rows(chunk(c), KC, c * KC)

    def count(pred):
        def part(rows):
            one = jnp.where(pred(sc_ref[rows, :]), 1.0, 0.0)
            return jnp.sum(one.reshape(-1, COUNT_ROWS, TQ), axis=0)

        cnt = lax.fori_loop(0, n_pairs, lambda c2, cnt: cnt + part(pair(c2)), jnp.zeros((COUNT_ROWS, TQ), F32))
        cnt = lax.fori_loop(2 * n_pairs, n_chunks, lambda c, cnt: cnt + part(chunk(c)), cnt)
        return jnp.sum(cnt, axis=0, keepdims=True)

    def count_high(thr_h):
        def part(rows):
            one = jnp.where(sh_ref[rows, :] >= thr_h, jnp.ones((), BF), jnp.zeros((), BF))
            one = one.reshape(-1, COUNT_ROWS, TQ)
            tot = one[0]
            for g in range(1, one.shape[0]):
                tot = tot + one[g]
            return tot

        cnt = lax.fori_loop(0, n_pairs, lambda c2, cnt: cnt + part(pair(c2)), jnp.zeros((COUNT_ROWS, TQ), BF))
        cnt = lax.fori_loop(2 * n_pairs, n_chunks, lambda c, cnt: cnt + part(chunk(c)), cnt)
        return jnp.sum(cnt.astype(F32), axis=0, keepdims=True)

    def key_to_f32(cu):
        ks = cu ^ jnp.int32(-2 ** 31)
        bits = jnp.where(ks < 0, ks ^ jnp.int32(2 ** 31 - 1), ks)
        return lax.bitcast_convert_type(bits, F32)

    def bit_pass(it, st, high=False):
        cu, done = st
        cand = cu | lax.shift_left(jnp.int32(1), 31 - it)
        thr_c = key_to_f32(cand)
        cnt = count_high(high_half(thr_c).astype(BF)) if high else count(lambda x: x >= thr_c)
        take = jnp.logical_and(cnt >= top_k, done == 0)
        return jnp.where(take, cand, cu), jnp.where(jnp.logical_and(take, cnt == top_k), 1, done)

    def bits_left(st):
        it, _, done = st
        return jnp.logical_and(it < 32, jnp.min(done) == 0)

    def two_passes(st):
        it, cu, done = st
        cu, done = lax.fori_loop(it, it + 2, bit_pass, (cu, done))
        return it + 2, cu, done

    st = lax.fori_loop(0, 16, functools.partial(bit_pass, high=True),
                       (jnp.zeros((1, TQ), jnp.int32), jnp.zeros((1, TQ), jnp.int32)))
    st = lax.fori_loop(16, UNTESTED_BITS, bit_pass, st)
    _, cu, _ = lax.while_loop(bits_left, two_passes, (jnp.int32(UNTESTED_BITS),) + st)
    thr = jnp.where((cu & jnp.int32(-2 ** 23)) == 0, -jnp.inf, key_to_f32(cu))
    n_ge = count(lambda x: x >= thr)
    has_tie = jnp.max(n_ge) > top_k

    @pl.when(jnp.logical_not(has_tie))
    def _select():
        def body(c, carry):
            sc_ref[chunk(c), :] = jnp.where(sc_ref[chunk(c), :] >= thr, 0.0, NEG_BIAS)
            return carry
        lax.fori_loop(0, n_chunks, body, 0)

    @pl.when(has_tie)
    def _select_ties():
        need = top_k - count(lambda x: x > thr)
        ltri = jnp.where(lax.broadcasted_iota(jnp.int32, (KC, KC), 1)
                         <= lax.broadcasted_iota(jnp.int32, (KC, KC), 0), 1.0, 0.0).astype(BF)

        def body(c, seen):
            x = sc_ref[chunk(c), :]
            eq = x == thr
            rank = jnp.dot(ltri, jnp.where(eq, 1.0, 0.0).astype(BF), preferred_element_type=F32) + seen
            keep = jnp.logical_or(x > thr, jnp.logical_and(eq, rank <= need))
            sc_ref[chunk(c), :] = jnp.where(keep, 0.0, NEG_BIAS)
            return rank[KC - 1:KC, :]
        lax.fori_loop(0, n_chunks, body, jnp.zeros((1, TQ), F32))

    m_ref[...] = jnp.full(m_ref.shape, -jnp.inf, F32)
    acc_ref[...] = jnp.zeros_like(acc_ref)

    def att_body(c, carry):
        bias = sc_ref[chunk(c), :]
        for h in range(ATT_HEADS):
            s = jnp.dot(k_ref[h, chunk(c), :], qt_ref[h], preferred_element_type=F32) + bias
            s_ref[h] = s
            m_old = m_ref[h:h + 1, :]
            m_new = jnp.maximum(m_old, jnp.max(s, axis=0, keepdims=True))
            al_ref[h:h + 1, :] = jnp.exp2(m_old - m_new)
            m_ref[h:h + 1, :] = m_new
        for h in range(ATT_HEADS):
            p = jnp.exp2(s_ref[h] - m_ref[h:h + 1, :])
            acc_ref[h] = al_ref[h:h + 1, :] * acc_ref[h] + jnp.dot(
                vt_ref[h, c], p.astype(BF), preferred_element_type=F32)
        return carry

    lax.fori_loop(0, n_chunks, att_body, 0)
    for h in range(ATT_HEADS):
        o_t = acc_ref[h, :HEAD_DIM, :] / acc_ref[h, HEAD_DIM:HEAD_DIM + 1, :]
        o_ref[:, h * HEAD_DIM:(h + 1) * HEAD_DIM] = o_t.T.astype(o_ref.dtype)


def _dsa(qt, k3, vt4, qit, ki, wt):
    s = ki.shape[0]
    top_k = min(INDEX_TOPK, s // 4)
    resident = pl.Buffered(1)
    return pl.pallas_call(
        functools.partial(_dsa_kernel, top_k=top_k),
        out_shape=jax.ShapeDtypeStruct((s, ATT_WIDTH), BF),
        grid=(s // TQ,),
        in_specs=[pl.BlockSpec((IDX_HEADS * IDX_DIM, TQ), lambda i: (0, i)),
                  pl.BlockSpec((IDX_HEADS, TQ), lambda i: (0, i)),
                  pl.BlockSpec((s, IDX_DIM), lambda i: (0, 0), pipeline_mode=resident),
                  pl.BlockSpec((ATT_HEADS, HEAD_DIM, TQ), lambda i: (0, 0, i)),
                  pl.BlockSpec((ATT_HEADS, s, HEAD_DIM), lambda i: (0, 0, 0), pipeline_mode=resident),
                  pl.BlockSpec((ATT_HEADS, s // KC, V_ROWS, KC), lambda i: (0, 0, 0, 0),
                               pipeline_mode=resident)],
        out_specs=pl.BlockSpec((TQ, ATT_WIDTH), lambda i: (i, 0)),
        scratch_shapes=[pltpu.VMEM((s, TQ), F32),
                        pltpu.VMEM((s, TQ), BF),
                        pltpu.VMEM((ATT_HEADS, TQ), F32),
                        pltpu.VMEM((ATT_HEADS, TQ), F32),
                        pltpu.VMEM((ATT_HEADS, V_ROWS, TQ), F32),
                        pltpu.VMEM((ATT_HEADS, KC, TQ), F32)],
        compiler_params=_cparams("arbitrary"),
        name="dsa",
    )(qit, wt, ki, qt, k3, vt4)


def _layer_norm(r, g, b):
    mu = jnp.mean(r, -1, keepdims=True)
    d = r - mu
    var = jnp.mean(d * d, -1, keepdims=True)
    return d * lax.rsqrt(var + LN_EPS) * g + b


def _out_ln_kernel(a1_ref, a2_ref, w1_ref, w2_ref, x_ref, g_ref, b_ref, xo_ref, xb_ref):
    y = (jnp.dot(a1_ref[...], w1_ref[...], preferred_element_type=F32)
         + jnp.dot(a2_ref[...], w2_ref[...], preferred_element_type=F32))
    o = _layer_norm(DN_ALPHA * x_ref[...] + y, g_ref[...], b_ref[...])
    xo_ref[...] = o
    xb_ref[...] = o.astype(xb_ref.dtype)


def _out_ln(o_gdn, o_dsa, w_out, xf, g, b, layer):
    s = xf.shape[0]
    tm = min(512, s)
    row = lambda i: (i, 0)
    lsel3 = lambda i: (layer, 0, 0)
    return pl.pallas_call(
        _out_ln_kernel,
        out_shape=(jax.ShapeDtypeStruct((s, D_MODEL), F32), jax.ShapeDtypeStruct((s, D_MODEL), BF)),
        grid=(s // tm,),
        in_specs=[pl.BlockSpec((tm, GDN_WIDTH), row), pl.BlockSpec((tm, ATT_WIDTH), row),
                  pl.BlockSpec((None, GDN_WIDTH, D_MODEL), lsel3, pipeline_mode=pl.Buffered(1)),
                  pl.BlockSpec((None, ATT_WIDTH, D_MODEL), lambda i: (layer, 1, 0),
                               pipeline_mode=pl.Buffered(1)),
                  pl.BlockSpec((tm, D_MODEL), row),
                  pl.BlockSpec((None, 1, D_MODEL), lsel3), pl.BlockSpec((None, 1, D_MODEL), lsel3)],
        out_specs=(pl.BlockSpec((tm, D_MODEL), row), pl.BlockSpec((tm, D_MODEL), row)),
        compiler_params=_cparams("parallel"),
        name="out_ln",
    )(o_gdn, o_dsa, w_out, w_out, xf, g, b)


def _ffn_up_kernel(x_ref, wg_ref, wv_ref, cg_ref, cv_ref, bg_ref, bv_ref, h_ref, carry_ref, wb_ref):
    tm, tn = h_ref.shape

    @pl.when(pl.program_id(1) == 0)
    def _init():
        carry_ref[...] = jnp.zeros_like(carry_ref)
        wb_ref[:, :tn] = wg_ref[...].astype(wb_ref.dtype)
        wb_ref[:, tn:] = wv_ref[...].astype(wb_ref.dtype)

    row8 = lax.broadcasted_iota(jnp.int32, (SUBLANES, h_ref.shape[1]), 0)

    def conv(u, prev, w, b):
        acc = u * w[FFN_CONV - 1:FFN_CONV, :] + b
        for s in range(1, FFN_CONV):
            us = pltpu.roll(u, s, axis=0)
            top = jnp.where(row8 < s, pltpu.roll(prev, s, axis=0), us[0:SUBLANES, :])
            us = jnp.concatenate([top, us[SUBLANES:, :]], axis=0)
            acc = acc + us * w[FFN_CONV - 1 - s:FFN_CONV - s, :]
        return acc

    x = x_ref[...]
    u = jnp.dot(x, wb_ref[...], preferred_element_type=F32)
    ug = u[:, :tn]
    uv = u[:, tn:]
    gate = conv(ug, carry_ref[0], cg_ref[...], bg_ref[...])
    val = conv(uv, carry_ref[1], cv_ref[...], bv_ref[...])
    carry_ref[0] = ug[tm - SUBLANES:, :]
    carry_ref[1] = uv[tm - SUBLANES:, :]
    h_ref[...] = (gate * _sigmoid(gate) * val).astype(h_ref.dtype)


def _ffn_up(xb, w_up, conv_w, conv_b, layer):
    s = xb.shape[0]
    tm = min(512, s)
    tn = 512
    nj = D_FF // tn
    return pl.pallas_call(
        _ffn_up_kernel,
        out_shape=jax.ShapeDtypeStruct((s, D_FF), BF),
        grid=(nj, s // tm),
        in_specs=[pl.BlockSpec((tm, D_MODEL), lambda j, i: (i, 0)),
                  pl.BlockSpec((None, D_MODEL, tn), lambda j, i: (layer, 0, j)),
                  pl.BlockSpec((None, D_MODEL, tn), lambda j, i: (layer, 0, j + nj)),
                  pl.BlockSpec((None, FFN_CONV, tn), lambda j, i: (layer, 0, j)),
                  pl.BlockSpec((None, FFN_CONV, tn), lambda j, i: (layer, 0, j + nj)),
                  pl.BlockSpec((None, 1, tn), lambda j, i: (layer, 0, j)),
                  pl.BlockSpec((None, 1, tn), lambda j, i: (layer, 0, j + nj))],
        out_specs=pl.BlockSpec((tm, tn), lambda j, i: (i, j)),
        scratch_shapes=[pltpu.VMEM((2, SUBLANES, tn), F32),
                        pltpu.VMEM((D_MODEL, 2 * tn), BF)],
        compiler_params=_cparams("arbitrary", "arbitrary"),
        name="ffn_up",
    )(xb, w_up, w_up, conv_w, conv_w, conv_b, conv_b)


def _ffn_down_kernel(h_ref, w_ref, x_ref, g_ref, b_ref, xo_ref, xb_ref):
    f = jnp.dot(h_ref[...], w_ref[...], preferred_element_type=F32)
    o = _layer_norm(DN_ALPHA * x_ref[...] + f, g_ref[...], b_ref[...])
    xo_ref[...] = o
    xb_ref[...] = o.astype(xb_ref.dtype)


def _ffn_down_ln(h, w_down, xf, g, b, layer):
    s = xf.shape[0]
    tm = min(256, s)
    row = lambda i: (i, 0)
    lsel3 = lambda i: (layer, 0, 0)
    return pl.pallas_call(
        _ffn_down_kernel,
        out_shape=(jax.ShapeDtypeStruct((s, D_MODEL), F32), jax.ShapeDtypeStruct((s, D_MODEL), BF)),
        grid=(s // tm,),
        in_specs=[pl.BlockSpec((tm, D_FF), row),
                  pl.BlockSpec((None, D_FF, D_MODEL), lsel3, pipeline_mode=pl.Buffered(1)),
                  pl.BlockSpec((tm, D_MODEL), row),
                  pl.BlockSpec((None, 1, D_MODEL), lsel3), pl.BlockSpec((None, 1, D_MODEL), lsel3)],
        out_specs=(pl.BlockSpec((tm, D_MODEL), row), pl.BlockSpec((tm, D_MODEL), row)),
        compiler_params=_cparams("parallel"),
        name="ffn_down",
    )(h, w_down, xf, g, b)


def _rope_tables(seq, dim):
    inv = ROPE_THETA ** (-jnp.arange(0, dim, 2, dtype=F32) / dim)
    ang = jnp.arange(seq, dtype=F32)[:, None] * inv[None, :]
    ang = jnp.concatenate([ang, ang], -1)
    return jnp.cos(ang), jnp.sin(ang)


def _lane_pad(v, offset):
    out = jnp.zeros((v.shape[0], 1, LANES), F32)
    return out.at[:, 0, offset:offset + v.shape[1]].set(v.astype(F32))


def kernel(x, w_in, gdn_conv_w, gdn_a_log, gdn_dt_bias, gdn_norm_w, kv_norm_w, w_ukv, idx_k_norm_g,
           idx_k_norm_b, w_out, ln1_g, ln1_b, ffn_up, ffn_conv_w, ffn_conv_b, ffn_down, ln2_g, ln2_b):
    seq = x.shape[1]
    depth = w_in.shape[0]

    o_ga = MAIN_WIDTH
    o_aq = o_ga + 2 * GDN_HEADS
    o_ckv = o_aq + ATT_WIDTH
    o_iq = o_ckv + KV_RANK
    o_ik = o_iq + IDX_HEADS * IDX_DIM
    o_iw = o_ik + IDX_DIM
    o_end = o_iw + IDX_HEADS
    w_main = w_in[:, :, :o_ga].astype(BF)
    w_rest = w_in[:, :, o_ga:].astype(BF)
    cut = lambda lo, hi: w_rest[:, :, lo - o_ga:hi - o_ga]
    w_tail = jnp.concatenate(
        [cut(o_aq, o_ckv), cut(o_iq, o_ik), cut(o_ckv, o_iq), cut(o_ik, o_end), cut(o_ga, o_aq),
         jnp.zeros((depth, D_MODEL, TAIL_WIDTH - (o_end - o_ga)), BF)], axis=-1)
    wk = w_ukv[:, :, :ATT_WIDTH].astype(BF)
    wvt = jnp.swapaxes(w_ukv[:, :, ATT_WIDTH:], 1, 2).astype(BF)
    w_out_b = w_out.astype(BF)
    w_down_b = ffn_down.astype(BF)

    arow = _lane_pad(gdn_a_log, SM_GA)
    drow = _lane_pad(gdn_dt_bias, SM_GA)
    acol = jnp.swapaxes(arow, 1, 2)
    dcol = jnp.swapaxes(drow, 1, 2)
    ig = _lane_pad(idx_k_norm_g, 0)
    ib = _lane_pad(idx_k_norm_b, 0)
    r3 = lambda v: v.astype(F32)[:, None, :]

    cos, sin = _rope_tables(seq, HEAD_DIM)
    icos, isin = _rope_tables(seq, IDX_DIM)
    sign = jnp.where(jnp.arange(HEAD_DIM) < HEAD_DIM // 2, -1.0, 1.0).astype(F32)
    lane_fill = lambda t: jnp.concatenate([t, jnp.zeros_like(t)], axis=-1)
    tabs = (cos, sin * sign, cos.T, sin.T, icos.T, isin.T, lane_fill(icos), lane_fill(isin))

    masks, tri = _gdn_masks()

    xf = x[0]
    xb = xf.astype(BF)
    for layer in range(depth):
        pm = _proj_main(xb, w_main, layer)
        pt = _proj_tail(xb, w_tail, layer)
        o_gdn = _gdn(pm, pt, gdn_conv_w, arow, acol, drow, dcol, r3(gdn_norm_w), masks, tri, layer)
        qt, k3, vt4, qit, ki, wt = _dsa_prep(pt, wk, wvt, r3(kv_norm_w), ig, ib, tabs, layer)
        o_dsa = _dsa(qt, k3, vt4, qit, ki, wt)
        xf, xb = _out_ln(o_gdn, o_dsa, w_out_b, xf, r3(ln1_g), r3(ln1_b), layer)
        h = _ffn_up(xb, ffn_up, ffn_conv_w, r3(ffn_conv_b), layer)
        xf, xb = _ffn_down_ln(h, w_down_b, xf, r3(ln2_g), r3(ln2_b), layer)
    return xf[None]
```

```python
import functools

import jax
import jax.numpy as jnp
import numpy as np
from jax import lax
from jax.experimental import pallas as pl
from jax.experimental.pallas import tpu as pltpu

D_MODEL = 2048
DEPTH = 4
HEAD_DIM = 128
GDN_HEADS = 8
ATT_HEADS = 8
GDN_WIDTH = GDN_HEADS * HEAD_DIM
ATT_WIDTH = ATT_HEADS * HEAD_DIM
GDN_CONV = 4
KV_RANK = 256
IDX_HEADS = 16
IDX_DIM = 64
INDEX_TOPK = 256
ROPE_THETA = 10000.0
D_FF = 5632
FFN_CONV = 3
LN_EPS = 1e-5
RMS_EPS = 1e-6
DN_ALPHA = (2 * DEPTH) ** 0.25

BF = jnp.bfloat16
F32 = jnp.float32
HIGHEST = lax.Precision.HIGHEST

LANES = 128
SUBLANES = 8
VMEM_LIMIT_BYTES = 56 * 1024 * 1024

MAIN_WIDTH = 4 * GDN_WIDTH
COL_GDN_QKV = 0
COL_GDN_Z = 3 * GDN_WIDTH
COL_ATT_Q = 0
COL_IDX_Q = COL_ATT_Q + ATT_WIDTH
COL_KV = COL_IDX_Q + IDX_HEADS * IDX_DIM
COL_SMALL = COL_KV + KV_RANK
SM_IW = IDX_DIM
SM_GA = SM_IW + IDX_HEADS
SM_GB = SM_GA + GDN_HEADS
TAIL_WIDTH = COL_SMALL + 2 * LANES

GDN_TILE = 256
INV_BASE = 16
TQ = 256
KC = 256
COUNT_ROWS = 64
UNTESTED_BITS = 22
NEG_BIAS = -1e30
V_ROWS = HEAD_DIM + 16
LOG2E = 1.4426950408889634


def _cparams(*sem):
    return pltpu.CompilerParams(dimension_semantics=sem, vmem_limit_bytes=VMEM_LIMIT_BYTES)


def _sigmoid(x):
    return 1.0 / (1.0 + jnp.exp(-x))


def _softplus(x):
    return jnp.maximum(x, 0.0) + jnp.log(1.0 + jnp.exp(-jnp.abs(x)))


def _dot(a, b):
    return jnp.dot(a.astype(BF), b.astype(BF), preferred_element_type=F32)


def _mm_kernel(a_ref, b_ref, o_ref):
    o_ref[...] = jnp.dot(a_ref[...], b_ref[...], preferred_element_type=F32)


def _proj_main(xb, w_main, layer):
    s, k = xb.shape
    tm = min(1024, s)
    tn = 1024
    return pl.pallas_call(
        _mm_kernel,
        out_shape=jax.ShapeDtypeStruct((s, MAIN_WIDTH), F32),
        grid=(MAIN_WIDTH // tn, s // tm),
        in_specs=[pl.BlockSpec((tm, k), lambda j, i: (i, 0)),
                  pl.BlockSpec((None, k, tn), lambda j, i: (layer, 0, j))],
        out_specs=pl.BlockSpec((tm, tn), lambda j, i: (i, j)),
        compiler_params=_cparams("parallel", "parallel"),
        name="proj_main",
    )(xb, w_main)


def _proj_tail(xb, w_tail, layer):
    s, k = xb.shape
    tm = min(1024, s)
    tn = TAIL_WIDTH // 2
    return pl.pallas_call(
        _mm_kernel,
        out_shape=jax.ShapeDtypeStruct((s, TAIL_WIDTH), F32),
        grid=(TAIL_WIDTH // tn, s // tm),
        in_specs=[pl.BlockSpec((tm, k), lambda j, i: (i, 0)),
                  pl.BlockSpec((None, k, tn), lambda j, i: (layer, 0, j))],
        out_specs=pl.BlockSpec((tm, tn), lambda j, i: (i, j)),
        compiler_params=_cparams("parallel", "parallel"),
        name="proj_tail",
    )(xb, w_tail)


M_LOWER, M_STRICT, M_EYE, M_BASE, M_MERGE0 = 0, 1, 2, 3, 4
N_MERGE = int(np.log2(GDN_TILE // INV_BASE))


def _gdn_masks():
    r = np.arange(GDN_TILE)[:, None]
    c = np.arange(GDN_TILE)[None, :]
    same = lambda size: (r // size) == (c // size)
    rows = [c <= r, c < r, c == r, same(INV_BASE)]
    rows += [same(2 * INV_BASE << i) & ~same(INV_BASE << i) for i in range(N_MERGE)]
    masks = jnp.asarray(np.stack(rows).astype(np.float32))
    tri = jnp.asarray(np.stack([c <= r, r <= c]).astype(np.float32)).astype(BF)
    return masks, tri


def _split3(x):
    h1 = x.astype(BF)
    r1 = x - h1.astype(F32)
    h2 = r1.astype(BF)
    return h1, h2, (r1 - h2.astype(F32)).astype(BF)


def _unit_lower_inverses(a, mask_ref):
    nk = [-(x * mask_ref[M_BASE]) for x in a]
    t = [mask_ref[M_EYE] + x for x in nk]
    for _ in range(int(np.log2(INV_BASE)) - 1):
        nk = [_dot(x, x) for x in nk]
        t = [x + _dot(x, y) for x, y in zip(t, nk)]
    for lvl in range(N_MERGE):
        et = [_dot(x * mask_ref[M_MERGE0 + lvl], y) for x, y in zip(a, t)]
        t = [x - _dot(x, y) for x, y in zip(t, et)]
    return t


def _gdn_kernel(qkv_ref, z_ref, sm_ref, cw_ref, arow_ref, acol_ref, drow_ref, dcol_ref, nw_ref, mask_ref, tri_ref,
                o_ref, carry_ref, state_ref):
    n = GDN_TILE
    heads = range(GDN_HEADS)

    @pl.when(pl.program_id(0) == 0)
    def _init():
        carry_ref[...] = jnp.zeros_like(carry_ref)
        state_ref[...] = jnp.zeros_like(state_ref)

    row8 = lax.broadcasted_iota(jnp.int32, (SUBLANES, LANES), 0)

    sm = sm_ref[...]
    g_cols = -jnp.exp(arow_ref[...]) * _softplus(sm + drow_ref[...])
    beta_cols = _sigmoid(sm)
    sm_t = sm.T
    ga = slice(SM_GA, SM_GA + GDN_HEADS)
    g_rows = -jnp.exp(acol_ref[ga, :]) * _softplus(sm_t[ga, :] + dcol_ref[ga, :])
    gc_cols = sum(jnp.dot(tri_ref[0], part, preferred_element_type=F32) for part in _split3(g_cols))
    gc_rows = sum(jnp.dot(part, tri_ref[1], preferred_element_type=F32) for part in _split3(g_rows))

    def conv_silu(off):
        x = qkv_ref[:, off:off + LANES]
        prev = carry_ref[:, off:off + LANES]
        w = cw_ref[:, off:off + LANES]
        acc = x * w[GDN_CONV - 1:GDN_CONV, :]
        for s in range(1, GDN_CONV):
            xs = pltpu.roll(x, s, axis=0)
            top = jnp.where(row8 < s, pltpu.roll(prev, s, axis=0), xs[0:SUBLANES, :])
            xs = jnp.concatenate([top, xs[SUBLANES:, :]], axis=0)
            acc = acc + xs * w[GDN_CONV - 1 - s:GDN_CONV - s, :]
        return acc * _sigmoid(acc)

    q = [conv_silu(h * HEAD_DIM) for h in heads]
    k = [conv_silu(GDN_WIDTH + h * HEAD_DIM) for h in heads]
    v = [conv_silu(2 * GDN_WIDTH + h * HEAD_DIM) for h in heads]
    q = [x * lax.rsqrt(jnp.sum(x * x, -1, keepdims=True) + RMS_EPS) * (HEAD_DIM ** -0.5) for x in q]
    k = [x * lax.rsqrt(jnp.sum(x * x, -1, keepdims=True) + RMS_EPS) for x in k]
    gcol = [gc_cols[:, SM_GA + h:SM_GA + h + 1] for h in heads]
    grow = [gc_rows[h:h + 1, :] for h in heads]
    bcol = [beta_cols[:, SM_GB + h:SM_GB + h + 1] for h in heads]
    glast = [x[n - 1:n, :] for x in gcol]
    ecol = [jnp.exp(x) for x in gcol]
    decay = [mask_ref[M_LOWER] * jnp.exp(jnp.minimum(c - r, 0.0)) for c, r in zip(gcol, grow)]
    k_t = [x.T for x in k]
    k_tb = [x.astype(BF) for x in k_t]
    kk = [jnp.dot(x.astype(BF), y, preferred_element_type=F32) for x, y in zip(k, k_tb)]
    qk = [jnp.dot(x.astype(BF), y, preferred_element_type=F32) for x, y in zip(q, k_tb)]
    a = [mask_ref[M_STRICT] * (b * x * d) for b, x, d in zip(bcol, kk, decay)]
    t = _unit_lower_inverses(a, mask_ref)
    rhs = [jnp.concatenate([v[h] * bcol[h], k[h] * (bcol[h] * ecol[h])], axis=1) for h in heads]
    sol = [_dot(x, y) for x, y in zip(t, rhs)]
    state = [state_ref[h] for h in heads]
    state_b = [x.astype(BF) for x in state]
    v_new = [x[:, :HEAD_DIM] - jnp.dot(x[:, HEAD_DIM:].astype(BF), s, preferred_element_type=F32)
             for x, s in zip(sol, state_b)]
    v_nb = [x.astype(BF) for x in v_new]
    o = [jnp.dot((q[h] * ecol[h]).astype(BF), state_b[h], preferred_element_type=F32)
         + jnp.dot((qk[h] * decay[h]).astype(BF), v_nb[h], preferred_element_type=F32) for h in heads]
    for h in heads:
        k_dec_t = k_t[h] * jnp.exp(glast[h] - grow[h])
        state_ref[h] = state[h] * jnp.exp(glast[h]) + jnp.dot(k_dec_t.astype(BF), v_nb[h],
                                                              preferred_element_type=F32)
    for h in heads:
        on = o[h] * lax.rsqrt(jnp.mean(o[h] * o[h], -1, keepdims=True) + RMS_EPS) * nw_ref[...]
        z = z_ref[:, h * HEAD_DIM:(h + 1) * HEAD_DIM]
        o_ref[:, h * HEAD_DIM:(h + 1) * HEAD_DIM] = (on * (z * _sigmoid(z))).astype(o_ref.dtype)

    carry_ref[...] = qkv_ref[n - SUBLANES:n, :]


def _gdn(pm, pt, conv_w, arow, acol, drow, dcol, norm_w, masks, tri, layer):
    s = pm.shape[0]
    n = GDN_TILE
    lsel3 = lambda i: (layer, 0, 0)
    const3 = lambda i: (0, 0, 0)
    return pl.pallas_call(
        _gdn_kernel,
        out_shape=jax.ShapeDtypeStruct((s, GDN_WIDTH), BF),
        grid=(s // n,),
        in_specs=[pl.BlockSpec((n, 3 * GDN_WIDTH), lambda i: (i, COL_GDN_QKV // (3 * GDN_WIDTH))),
                  pl.BlockSpec((n, GDN_WIDTH), lambda i: (i, COL_GDN_Z // GDN_WIDTH)),
                  pl.BlockSpec((n, LANES), lambda i: (i, COL_SMALL // LANES)),
                  pl.BlockSpec((None, GDN_CONV, 3 * GDN_WIDTH), lsel3),
                  pl.BlockSpec((None, 1, LANES), lsel3),
                  pl.BlockSpec((None, LANES, 1), lsel3),
                  pl.BlockSpec((None, 1, LANES), lsel3),
                  pl.BlockSpec((None, LANES, 1), lsel3),
                  pl.BlockSpec((None, 1, HEAD_DIM), lsel3),
                  pl.BlockSpec(masks.shape, const3, pipeline_mode=pl.Buffered(1)),
                  pl.BlockSpec(tri.shape, const3, pipeline_mode=pl.Buffered(1))],
        out_specs=pl.BlockSpec((n, GDN_WIDTH), lambda i: (i, 0)),
        scratch_shapes=[pltpu.VMEM((SUBLANES, 3 * GDN_WIDTH), F32),
                        pltpu.VMEM((GDN_HEADS, HEAD_DIM, HEAD_DIM), F32)],
        compiler_params=_cparams("arbitrary"),
        name="gdn",
    )(pm, pm, pt, conv_w, arow, acol, drow, dcol, norm_w, masks, tri)


def _dsa_prep_kernel(aq_ref, iq_ref, ckv_ref, sm_ref, wk_ref, wvt_ref, kvg_ref, ig_ref, ib_ref,
                     cos_ref, sins_ref, cos_t_ref, sin_t_ref, icos_t_ref, isin_t_ref, icos_ref, isin_ref,
                     qt_ref, k_ref, vt_ref, qit_ref, ki_ref, wt_ref):
    half = HEAD_DIM // 2
    ihalf = IDX_DIM // 2

    aq_t = aq_ref[...].T
    cos_t = cos_t_ref[...]
    sin_t = sin_t_ref[...]
    for h in range(ATT_HEADS):
        x = aq_t[h * HEAD_DIM:(h + 1) * HEAD_DIM, :]
        rot = jnp.concatenate([-x[half:, :], x[:half, :]], axis=0)
        qt_ref[h] = ((x * cos_t + rot * sin_t) * (LOG2E * HEAD_DIM ** -0.5)).astype(qt_ref.dtype)

    iq_t = iq_ref[...].T
    icos_t = icos_t_ref[...]
    isin_t = isin_t_ref[...]
    for h in range(IDX_HEADS):
        x = iq_t[h * IDX_DIM:(h + 1) * IDX_DIM, :]
        rot = jnp.concatenate([-x[ihalf:, :], x[:ihalf, :]], axis=0)
        qit_ref[h * IDX_DIM:(h + 1) * IDX_DIM, :] = (
            (x * icos_t + rot * isin_t) * (IDX_DIM ** -0.5)).astype(qit_ref.dtype)

    c = ckv_ref[...]
    kvn = c * lax.rsqrt(jnp.mean(c * c, -1, keepdims=True) + RMS_EPS) * kvg_ref[...]
    k = jnp.dot(kvn.astype(BF), wk_ref[...], preferred_element_type=F32)
    cos = cos_ref[...]
    sins = sins_ref[...]
    for h in range(ATT_HEADS):
        x = k[:, h * HEAD_DIM:(h + 1) * HEAD_DIM]
        k_ref[h] = (x * cos + pltpu.roll(x, half, axis=1) * sins).astype(k_ref.dtype)
    v_t = jnp.dot(wvt_ref[...], kvn.T.astype(BF), preferred_element_type=F32)
    for h in range(ATT_HEADS):
        vt_ref[h, 0, :HEAD_DIM, :] = v_t[h * HEAD_DIM:(h + 1) * HEAD_DIM, :].astype(vt_ref.dtype)
        vt_ref[h, 0, HEAD_DIM:, :] = jnp.ones((V_ROWS - HEAD_DIM, v_t.shape[1]), vt_ref.dtype)

    sm = sm_ref[...]
    lane = lax.broadcasted_iota(jnp.int32, sm.shape, 1)
    is_k = lane < IDX_DIM
    mu = jnp.sum(jnp.where(is_k, sm, 0.0), -1, keepdims=True) * (1.0 / IDX_DIM)
    d = jnp.where(is_k, sm - mu, 0.0)
    var = jnp.sum(d * d, -1, keepdims=True) * (1.0 / IDX_DIM)
    kin = d * lax.rsqrt(var + LN_EPS) * ig_ref[...] + ib_ref[...]
    below = pltpu.roll(kin, ihalf, axis=1)
    above = pltpu.roll(kin, LANES - ihalf, axis=1)
    rot = jnp.where((lane & (IDX_DIM - 1)) < ihalf, -above, below)
    kir = kin * icos_ref[...] + rot * isin_ref[...]
    ki_ref[...] = kir[:, :IDX_DIM].astype(ki_ref.dtype)
    wt_ref[...] = sm.T[SM_IW:SM_IW + IDX_HEADS, :] * (IDX_HEADS ** -0.5)


def _dsa_prep(p, wk, wvt, kvg, ig, ib, tabs, layer):
    s = p.shape[0]
    tp = KC
    lsel3 = lambda i: (layer, 0, 0)
    row = lambda i: (i, 0)
    col = lambda i: (0, i)
    out_shape = (jax.ShapeDtypeStruct((ATT_HEADS, HEAD_DIM, s), BF),
                 jax.ShapeDtypeStruct((ATT_HEADS, s, HEAD_DIM), BF),
                 jax.ShapeDtypeStruct((ATT_HEADS, s // tp, V_ROWS, tp), BF),
                 jax.ShapeDtypeStruct((IDX_HEADS * IDX_DIM, s), BF),
                 jax.ShapeDtypeStruct((s, IDX_DIM), BF),
                 jax.ShapeDtypeStruct((IDX_HEADS, s), F32))
    return pl.pallas_call(
        _dsa_prep_kernel,
        out_shape=out_shape,
        grid=(s // tp,),
        in_specs=[pl.BlockSpec((tp, ATT_WIDTH), lambda i: (i, COL_ATT_Q // ATT_WIDTH)),
                  pl.BlockSpec((tp, IDX_HEADS * IDX_DIM), lambda i: (i, COL_IDX_Q // (IDX_HEADS * IDX_DIM))),
                  pl.BlockSpec((tp, KV_RANK), lambda i: (i, COL_KV // KV_RANK)),
                  pl.BlockSpec((tp, LANES), lambda i: (i, COL_SMALL // LANES)),
                  pl.BlockSpec((None, KV_RANK, ATT_WIDTH), lsel3),
                  pl.BlockSpec((None, ATT_WIDTH, KV_RANK), lsel3),
                  pl.BlockSpec((None, 1, KV_RANK), lsel3),
                  pl.BlockSpec((None, 1, LANES), lsel3),
                  pl.BlockSpec((None, 1, LANES), lsel3),
                  pl.BlockSpec((tp, HEAD_DIM), row), pl.BlockSpec((tp, HEAD_DIM), row),
                  pl.BlockSpec((HEAD_DIM, tp), col), pl.BlockSpec((HEAD_DIM, tp), col),
                  pl.BlockSpec((IDX_DIM, tp), col), pl.BlockSpec((IDX_DIM, tp), col),
                  pl.BlockSpec((tp, LANES), row), pl.BlockSpec((tp, LANES), row)],
        out_specs=(pl.BlockSpec((ATT_HEADS, HEAD_DIM, tp), lambda i: (0, 0, i)),
                   pl.BlockSpec((ATT_HEADS, tp, HEAD_DIM), lambda i: (0, i, 0)),
                   pl.BlockSpec((ATT_HEADS, 1, V_ROWS, tp), lambda i: (0, i, 0, 0)),
                   pl.BlockSpec((IDX_HEADS * IDX_DIM, tp), col),
                   pl.BlockSpec((tp, IDX_DIM), row),
                   pl.BlockSpec((IDX_HEADS, tp), col)),
        compiler_params=_cparams("parallel"),
        name="dsa_prep",
    )(p, p, p, p, wk, wvt, kvg, ig, ib, *tabs)


def _dsa_kernel(qit_ref, wt_ref, ki_ref, qt_ref, k_ref, vt_ref, o_ref, sc_ref, sh_ref, m_ref, al_ref, acc_ref,
                s_ref, *, top_k):
    i = pl.program_id(0)
    n_chunks = i + 1
    n_pairs = lax.shift_right_logical(n_chunks, 1)

    def chunk(c):
        return pl.ds(pl.multiple_of(c * KC, KC), KC)

    def pair(c2):
        return pl.ds(pl.multiple_of(c2 * (2 * KC), 2 * KC), 2 * KC)

    def high_half(v):
        return lax.bitcast_convert_type(lax.bitcast_convert_type(v, jnp.int32) & jnp.int32(-2 ** 16), F32)

    w_t = wt_ref[...]

    def score_rows(rows, n_rows, first_key):
        kic = ki_ref[rows, :]
        acc = jnp.zeros((n_rows, TQ), F32)
        for h in range(IDX_HEADS):
            d = jnp.dot(kic, qit_ref[h * IDX_DIM:(h + 1) * IDX_DIM, :], preferred_element_type=F32)
            acc = acc + w_t[h:h + 1, :] * jnp.maximum(d, 0.0)
        visible = (first_key + lax.broadcasted_iota(jnp.int32, (n_rows, TQ), 0)
                   <= i * TQ + lax.broadcasted_iota(jnp.int32, (n_rows, TQ), 1))
        score = jnp.where(visible, acc, jnp.nan)
        sc_ref[rows, :] = score
        sh_ref[rows, :] = high_half(score).astype(sh_ref.dtype)

    @pl.loop(0, n_pairs)
    def _score_pairs(c2):
        score_rows(pair(c2), 2 * KC, c2 * (2 * KC))

    @pl.loop(2 * n_pairs, n_chunks)
    def _score_rest(c):
        score_rows(chunk(c), KC, c * KC)

    def count(pred):
        def part(rows):
            one = jnp.where(pred(sc_ref[rows, :]), 1.0, 0.0)
            return jnp.sum(one.reshape(-1, COUNT_ROWS, TQ), axis=0)

        cnt = lax.fori_loop(0, n_pairs, lambda c2, cnt: cnt + part(pair(c2)), jnp.zeros((COUNT_ROWS, TQ), F32))
        cnt = lax.fori_loop(2 * n_pairs, n_chunks, lambda c, cnt: cnt + part(chunk(c)), cnt)
        return jnp.sum(cnt, axis=0, keepdims=True)

    def count_high(thr_h):
        def part(rows):
            one = jnp.where(sh_ref[rows, :] >= thr_h, jnp.ones((), BF), jnp.zeros((), BF))
            one = one.reshape(-1, COUNT_ROWS, TQ)
            tot = one[0]
            for g in range(1, one.shape[0]):
                tot = tot + one[g]
            return tot

        cnt = lax.fori_loop(0, n_pairs, lambda c2, cnt: cnt + part(pair(c2)), jnp.zeros((COUNT_ROWS, TQ), BF))
        cnt = lax.fori_loop(2 * n_pairs, n_chunks, lambda c, cnt: cnt + part(chunk(c)), cnt)
        return jnp.sum(cnt.astype(F32), axis=0, keepdims=True)

    def key_to_f32(cu):
        ks = cu ^ jnp.int32(-2 ** 31)
        bits = jnp.where(ks < 0, ks ^ jnp.int32(2 ** 31 - 1), ks)
        return lax.bitcast_convert_type(bits, F32)

    def bit_pass(it, st, high=False):
        cu, done = st
        cand = cu | lax.shift_left(jnp.int32(1), 31 - it)
        thr_c = key_to_f32(cand)
        cnt = count_high(high_half(thr_c).astype(BF)) if high else count(lambda x: x >= thr_c)
        take = jnp.logical_and(cnt >= top_k, done == 0)
        return jnp.where(take, cand, cu), jnp.where(jnp.logical_and(take, cnt == top_k), 1, done)

    def bits_left(st):
        it, _, done = st
        return jnp.logical_and(it < 32, jnp.min(done) == 0)

    def two_passes(st):
        it, cu, done = st
        cu, done = lax.fori_loop(it, it + 2, bit_pass, (cu, done))
        return it + 2, cu, done

    st = lax.fori_loop(0, 16, functools.partial(bit_pass, high=True),
                       (jnp.zeros((1, TQ), jnp.int32), jnp.zeros((1, TQ), jnp.int32)))
    st = lax.fori_loop(16, UNTESTED_BITS, bit_pass, st)
    _, cu, _ = lax.while_loop(bits_left, two_passes, (jnp.int32(UNTESTED_BITS),) + st)
    thr = jnp.where((cu & jnp.int32(-2 ** 23)) == 0, -jnp.inf, key_to_f32(cu))
    n_ge = count(lambda x: x >= thr)
    has_tie = jnp.max(n_ge) > top_k

    @pl.when(jnp.logical_not(has_tie))
    def _select():
        def body(c, carry):
            sc_ref[chunk(c), :] = jnp.where(sc_ref[chunk(c), :] >= thr, 0.0, NEG_BIAS)
            return carry
        lax.fori_loop(0, n_chunks, body, 0)

    @pl.when(has_tie)
    def _select_ties():
        need = top_k - count(lambda x: x > thr)
        ltri = jnp.where(lax.broadcasted_iota(jnp.int32, (KC, KC), 1)
                         <= lax.broadcasted_iota(jnp.int32, (KC, KC), 0), 1.0, 0.0).astype(BF)

        def body(c, seen):
            x = sc_ref[chunk(c), :]
            eq = x == thr
            rank = jnp.dot(ltri, jnp.where(eq, 1.0, 0.0).astype(BF), preferred_element_type=F32) + seen
            keep = jnp.logical_or(x > thr, jnp.logical_and(eq, rank <= need))
            sc_ref[chunk(c), :] = jnp.where(keep, 0.0, NEG_BIAS)
            return rank[KC - 1:KC, :]
        lax.fori_loop(0, n_chunks, body, jnp.zeros((1, TQ), F32))

    m_ref[...] = jnp.full(m_ref.shape, -jnp.inf, F32)
    acc_ref[...] = jnp.zeros_like(acc_ref)

    def att_body(c, carry):
        def logits(h):
            s = jnp.dot(k_ref[h, chunk(c), :], qt_ref[h], preferred_element_type=F32) + sc_ref[chunk(c), :]
            s_ref[h] = s
            m_old = m_ref[h:h + 1, :]
            m_new = jnp.maximum(m_old, jnp.max(s, axis=0, keepdims=True))
            al_ref[h:h + 1, :] = jnp.exp2(m_old - m_new)
            m_ref[h:h + 1, :] = m_new

        def values(h):
            p = jnp.exp2(s_ref[h] - m_ref[h:h + 1, :])
            acc_ref[h] = al_ref[h:h + 1, :] * acc_ref[h] + jnp.dot(
                vt_ref[h, c], p.astype(BF), preferred_element_type=F32)

        for h in range(ATT_HEADS):
            logits(h)
        for h in range(ATT_HEADS):
            values(h)
        return carry

    lax.fori_loop(0, n_chunks, att_body, 0)
    for h in range(ATT_HEADS):
        o_t = acc_ref[h, :HEAD_DIM, :] / acc_ref[h, HEAD_DIM:HEAD_DIM + 1, :]
        o_ref[:, h * HEAD_DIM:(h + 1) * HEAD_DIM] = o_t.T.astype(o_ref.dtype)


def _dsa(qt, k3, vt4, qit, ki, wt):
    s = ki.shape[0]
    top_k = min(INDEX_TOPK, s // 4)
    resident = pl.Buffered(1)
    return pl.pallas_call(
        functools.partial(_dsa_kernel, top_k=top_k),
        out_shape=jax.ShapeDtypeStruct((s, ATT_WIDTH), BF),
        grid=(s // TQ,),
        in_specs=[pl.BlockSpec((IDX_HEADS * IDX_DIM, TQ), lambda i: (0, i)),
                  pl.BlockSpec((IDX_HEADS, TQ), lambda i: (0, i)),
                  pl.BlockSpec((s, IDX_DIM), lambda i: (0, 0), pipeline_mode=resident),
                  pl.BlockSpec((ATT_HEADS, HEAD_DIM, TQ), lambda i: (0, 0, i)),
                  pl.BlockSpec((ATT_HEADS, s, HEAD_DIM), lambda i: (0, 0, 0), pipeline_mode=resident),
                  pl.BlockSpec((ATT_HEADS, s // KC, V_ROWS, KC), lambda i: (0, 0, 0, 0),
                               pipeline_mode=resident)],
        out_specs=pl.BlockSpec((TQ, ATT_WIDTH), lambda i: (i, 0)),
        scratch_shapes=[pltpu.VMEM((s, TQ), F32),
                        pltpu.VMEM((s, TQ), BF),
                        pltpu.VMEM((ATT_HEADS, TQ), F32),
                        pltpu.VMEM((ATT_HEADS, TQ), F32),
                        pltpu.VMEM((ATT_HEADS, V_ROWS, TQ), F32),
                        pltpu.VMEM((ATT_HEADS, KC, TQ), F32)],
        compiler_params=_cparams("arbitrary"),
        name="dsa",
    )(qit, wt, ki, qt, k3, vt4)


def _layer_norm(r, g, b):
    mu = jnp.mean(r, -1, keepdims=True)
    d = r - mu
    var = jnp.mean(d * d, -1, keepdims=True)
    return d * lax.rsqrt(var + LN_EPS) * g + b


def _out_ln_kernel(a1_ref, a2_ref, w1_ref, w2_ref, x_ref, g_ref, b_ref, xo_ref, xb_ref):
    y = (jnp.dot(a1_ref[...], w1_ref[...], preferred_element_type=F32)
         + jnp.dot(a2_ref[...], w2_ref[...], preferred_element_type=F32))
    o = _layer_norm(DN_ALPHA * x_ref[...] + y, g_ref[...], b_ref[...])
    xo_ref[...] = o
    xb_ref[...] = o.astype(xb_ref.dtype)


def _out_ln(o_gdn, o_dsa, w_out, xf, g, b, layer):
    s = xf.shape[0]
    tm = min(512, s)
    row = lambda i: (i, 0)
    lsel3 = lambda i: (layer, 0, 0)
    return pl.pallas_call(
        _out_ln_kernel,
        out_shape=(jax.ShapeDtypeStruct((s, D_MODEL), F32), jax.ShapeDtypeStruct((s, D_MODEL), BF)),
        grid=(s // tm,),
        in_specs=[pl.BlockSpec((tm, GDN_WIDTH), row), pl.BlockSpec((tm, ATT_WIDTH), row),
                  pl.BlockSpec((None, GDN_WIDTH, D_MODEL), lsel3, pipeline_mode=pl.Buffered(1)),
                  pl.BlockSpec((None, ATT_WIDTH, D_MODEL), lambda i: (layer, 1, 0),
                               pipeline_mode=pl.Buffered(1)),
                  pl.BlockSpec((tm, D_MODEL), row),
                  pl.BlockSpec((None, 1, D_MODEL), lsel3), pl.BlockSpec((None, 1, D_MODEL), lsel3)],
        out_specs=(pl.BlockSpec((tm, D_MODEL), row), pl.BlockSpec((tm, D_MODEL), row)),
        compiler_params=_cparams("parallel"),
        name="out_ln",
    )(o_gdn, o_dsa, w_out, w_out, xf, g, b)


def _ffn_up_kernel(x_ref, wg_ref, wv_ref, cg_ref, cv_ref, bg_ref, bv_ref, h_ref, carry_ref, wb_ref):
    tm, tn = h_ref.shape

    @pl.when(pl.program_id(1) == 0)
    def _init():
        carry_ref[...] = jnp.zeros_like(carry_ref)
        wb_ref[0] = wg_ref[...].astype(wb_ref.dtype)
        wb_ref[1] = wv_ref[...].astype(wb_ref.dtype)

    row8 = lax.broadcasted_iota(jnp.int32, (SUBLANES, h_ref.shape[1]), 0)

    def conv(u, prev, w, b):
        acc = u * w[FFN_CONV - 1:FFN_CONV, :] + b
        for s in range(1, FFN_CONV):
            us = pltpu.roll(u, s, axis=0)
            top = jnp.where(row8 < s, pltpu.roll(prev, s, axis=0), us[0:SUBLANES, :])
            us = jnp.concatenate([top, us[SUBLANES:, :]], axis=0)
            acc = acc + us * w[FFN_CONV - 1 - s:FFN_CONV - s, :]
        return acc

    x = x_ref[...]
    ug = jnp.dot(x, wb_ref[0], preferred_element_type=F32)
    uv = jnp.dot(x, wb_ref[1], preferred_element_type=F32)
    gate = conv(ug, carry_ref[0], cg_ref[...], bg_ref[...])
    val = conv(uv, carry_ref[1], cv_ref[...], bv_ref[...])
    carry_ref[0] = ug[tm - SUBLANES:, :]
    carry_ref[1] = uv[tm - SUBLANES:, :]
    h_ref[...] = (gate * _sigmoid(gate) * val).astype(h_ref.dtype)


def _ffn_up(xb, w_up, conv_w, conv_b, layer):
    s = xb.shape[0]
    tm = min(1024, s)
    tn = 512
    nj = D_FF // tn
    return pl.pallas_call(
        _ffn_up_kernel,
        out_shape=jax.ShapeDtypeStruct((s, D_FF), BF),
        grid=(nj, s // tm),
        in_specs=[pl.BlockSpec((tm, D_MODEL), lambda j, i: (i, 0)),
                  pl.BlockSpec((None, D_MODEL, tn), lambda j, i: (layer, 0, j)),
                  pl.BlockSpec((None, D_MODEL, tn), lambda j, i: (layer, 0, j + nj)),
                  pl.BlockSpec((None, FFN_CONV, tn), lambda j, i: (layer, 0, j)),
                  pl.BlockSpec((None, FFN_CONV, tn), lambda j, i: (layer, 0, j + nj)),
                  pl.BlockSpec((None, 1, tn), lambda j, i: (layer, 0, j)),
                  pl.BlockSpec((None, 1, tn), lambda j, i: (layer, 0, j + nj))],
        out_specs=pl.BlockSpec((tm, tn), lambda j, i: (i, j)),
        scratch_shapes=[pltpu.VMEM((2, SUBLANES, tn), F32),
                        pltpu.VMEM((2, D_MODEL, tn), BF)],
        compiler_params=_cparams("arbitrary", "arbitrary"),
        name="ffn_up",
    )(xb, w_up, w_up, conv_w, conv_w, conv_b, conv_b)


def _ffn_down_kernel(h_ref, w_ref, x_ref, g_ref, b_ref, xo_ref, xb_ref):
    f = jnp.dot(h_ref[...], w_ref[...], preferred_element_type=F32)
    o = _layer_norm(DN_ALPHA * x_ref[...] + f, g_ref[...], b_ref[...])
    xo_ref[...] = o
    xb_ref[...] = o.astype(xb_ref.dtype)


def _ffn_down_ln(h, w_down, xf, g, b, layer):
    s = xf.shape[0]
    tm = min(256, s)
    row = lambda i: (i, 0)
    lsel3 = lambda i: (layer, 0, 0)
    return pl.pallas_call(
        _ffn_down_kernel,
        out_shape=(jax.ShapeDtypeStruct((s, D_MODEL), F32), jax.ShapeDtypeStruct((s, D_MODEL), BF)),
        grid=(s // tm,),
        in_specs=[pl.BlockSpec((tm, D_FF), row),
                  pl.BlockSpec((None, D_FF, D_MODEL), lsel3, pipeline_mode=pl.Buffered(1)),
                  pl.BlockSpec((tm, D_MODEL), row),
                  pl.BlockSpec((None, 1, D_MODEL), lsel3), pl.BlockSpec((None, 1, D_MODEL), lsel3)],
        out_specs=(pl.BlockSpec((tm, D_MODEL), row), pl.BlockSpec((tm, D_MODEL), row)),
        compiler_params=_cparams("parallel"),
        name="ffn_down",
    )(h, w_down, xf, g, b)


def _rope_tables(seq, dim):
    inv = ROPE_THETA ** (-jnp.arange(0, dim, 2, dtype=F32) / dim)
    ang = jnp.arange(seq, dtype=F32)[:, None] * inv[None, :]
    ang = jnp.concatenate([ang, ang], -1)
    return jnp.cos(ang), jnp.sin(ang)


def _lane_pad(v, offset):
    out = jnp.zeros((v.shape[0], 1, LANES), F32)
    return out.at[:, 0, offset:offset + v.shape[1]].set(v.astype(F32))


def kernel(x, w_in, gdn_conv_w, gdn_a_log, gdn_dt_bias, gdn_norm_w, kv_norm_w, w_ukv, idx_k_norm_g,
           idx_k_norm_b, w_out, ln1_g, ln1_b, ffn_up, ffn_conv_w, ffn_conv_b, ffn_down, ln2_g, ln2_b):
    seq = x.shape[1]
    depth = w_in.shape[0]

    o_ga = MAIN_WIDTH
    o_aq = o_ga + 2 * GDN_HEADS
    o_ckv = o_aq + ATT_WIDTH
    o_iq = o_ckv + KV_RANK
    o_ik = o_iq + IDX_HEADS * IDX_DIM
    o_iw = o_ik + IDX_DIM
    o_end = o_iw + IDX_HEADS
    w_main = w_in[:, :, :o_ga].astype(BF)
    w_rest = w_in[:, :, o_ga:].astype(BF)
    cut = lambda lo, hi: w_rest[:, :, lo - o_ga:hi - o_ga]
    w_tail = jnp.concatenate(
        [cut(o_aq, o_ckv), cut(o_iq, o_ik), cut(o_ckv, o_iq), cut(o_ik, o_end), cut(o_ga, o_aq),
         jnp.zeros((depth, D_MODEL, TAIL_WIDTH - (o_end - o_ga)), BF)], axis=-1)
    wk = w_ukv[:, :, :ATT_WIDTH].astype(BF)
    wvt = jnp.swapaxes(w_ukv[:, :, ATT_WIDTH:], 1, 2).astype(BF)
    w_out_b = w_out.astype(BF)
    w_down_b = ffn_down.astype(BF)

    arow = _lane_pad(gdn_a_log, SM_GA)
    drow = _lane_pad(gdn_dt_bias, SM_GA)
    acol = jnp.swapaxes(arow, 1, 2)
    dcol = jnp.swapaxes(drow, 1, 2)
    ig = _lane_pad(idx_k_norm_g, 0)
    ib = _lane_pad(idx_k_norm_b, 0)
    r3 = lambda v: v.astype(F32)[:, None, :]

    cos, sin = _rope_tables(seq, HEAD_DIM)
    icos, isin = _rope_tables(seq, IDX_DIM)
    sign = jnp.where(jnp.arange(HEAD_DIM) < HEAD_DIM // 2, -1.0, 1.0).astype(F32)
    lane_fill = lambda t: jnp.concatenate([t, jnp.zeros_like(t)], axis=-1)
    tabs = (cos, sin * sign, cos.T, sin.T, icos.T, isin.T, lane_fill(icos), lane_fill(isin))

    masks, tri = _gdn_masks()

    xf = x[0]
    xb = xf.astype(BF)
    for layer in range(depth):
        pm = _proj_main(xb, w_main, layer)
        pt = _proj_tail(xb, w_tail, layer)
        o_gdn = _gdn(pm, pt, gdn_conv_w, arow, acol, drow, dcol, r3(gdn_norm_w), masks, tri, layer)
        qt, k3, vt4, qit, ki, wt = _dsa_prep(pt, wk, wvt, r3(kv_norm_w), ig, ib, tabs, layer)
        o_dsa = _dsa(qt, k3, vt4, qit, ki, wt)
        xf, xb = _out_ln(o_gdn, o_dsa, w_out_b, xf, r3(ln1_g), r3(ln1_b), layer)
        h = _ffn_up(xb, ffn_up, ffn_conv_w, r3(ffn_conv_b), layer)
        xf, xb = _ffn_down_ln(h, w_down_b, xf, r3(ln2_g), r3(ln2_b), layer)
    return xf[None]
```

```python
import functools

import jax
import jax.numpy as jnp
import numpy as np
from jax import lax
from jax.experimental import pallas as pl
from jax.experimental.pallas import tpu as pltpu

D_MODEL = 2048
DEPTH = 4
HEAD_DIM = 128
GDN_HEADS = 8
ATT_HEADS = 8
GDN_WIDTH = GDN_HEADS * HEAD_DIM
ATT_WIDTH = ATT_HEADS * HEAD_DIM
GDN_CONV = 4
KV_RANK = 256
IDX_HEADS = 16
IDX_DIM = 64
INDEX_TOPK = 256
ROPE_THETA = 10000.0
D_FF = 5632
FFN_CONV = 3
LN_EPS = 1e-5
RMS_EPS = 1e-6
DN_ALPHA = (2 * DEPTH) ** 0.25

BF = jnp.bfloat16
F32 = jnp.float32
HIGHEST = lax.Precision.HIGHEST

LANES = 128
SUBLANES = 8
VMEM_LIMIT_BYTES = 58 * 1024 * 1024

MAIN_WIDTH = 4 * GDN_WIDTH
COL_GDN_QKV = 0
COL_GDN_Z = 3 * GDN_WIDTH
COL_ATT_Q = 0
COL_IDX_Q = COL_ATT_Q + ATT_WIDTH
COL_KV = COL_IDX_Q + IDX_HEADS * IDX_DIM
COL_SMALL = COL_KV + KV_RANK
SM_IW = IDX_DIM
SM_GA = SM_IW + IDX_HEADS
SM_GB = SM_GA + GDN_HEADS
TAIL_WIDTH = COL_SMALL + 2 * LANES

GDN_TILE = 256
INV_BASE = 16
TQ = 256
KC = 256
COUNT_ROWS = 64
UNTESTED_BITS = 22
NEG_BIAS = -1e30
V_ROWS = HEAD_DIM + 16
LOG2E = 1.4426950408889634


def _cparams(*sem):
    return pltpu.CompilerParams(dimension_semantics=sem, vmem_limit_bytes=VMEM_LIMIT_BYTES)


def _sigmoid(x):
    return 1.0 / (1.0 + jnp.exp(-x))


def _softplus(x):
    return jnp.maximum(x, 0.0) + jnp.log(1.0 + jnp.exp(-jnp.abs(x)))


def _dot(a, b):
    return jnp.dot(a.astype(BF), b.astype(BF), preferred_element_type=F32)


def _mm_kernel(a_ref, b_ref, o_ref):
    o_ref[...] = jnp.dot(a_ref[...], b_ref[...], preferred_element_type=F32)


def _proj_main(xb, w_main, layer):
    s, k = xb.shape
    tm = min(1024, s)
    tn = 1024
    return pl.pallas_call(
        _mm_kernel,
        out_shape=jax.ShapeDtypeStruct((s, MAIN_WIDTH), F32),
        grid=(MAIN_WIDTH // tn, s // tm),
        in_specs=[pl.BlockSpec((tm, k), lambda j, i: (i, 0)),
                  pl.BlockSpec((None, k, tn), lambda j, i: (layer, 0, j))],
        out_specs=pl.BlockSpec((tm, tn), lambda j, i: (i, j)),
        compiler_params=_cparams("parallel", "parallel"),
        name="proj_main",
    )(xb, w_main)


def _proj_tail(xb, w_tail, layer):
    s, k = xb.shape
    tm = min(1024, s)
    tn = TAIL_WIDTH // 2
    return pl.pallas_call(
        _mm_kernel,
        out_shape=jax.ShapeDtypeStruct((s, TAIL_WIDTH), F32),
        grid=(TAIL_WIDTH // tn, s // tm),
        in_specs=[pl.BlockSpec((tm, k), lambda j, i: (i, 0)),
                  pl.BlockSpec((None, k, tn), lambda j, i: (layer, 0, j))],
        out_specs=pl.BlockSpec((tm, tn), lambda j, i: (i, j)),
        compiler_params=_cparams("parallel", "parallel"),
        name="proj_tail",
    )(xb, w_tail)


M_LOWER, M_STRICT, M_EYE, M_BASE, M_MERGE0 = 0, 1, 2, 3, 4
N_MERGE = int(np.log2(GDN_TILE // INV_BASE))


def _gdn_masks():
    r = np.arange(GDN_TILE)[:, None]
    c = np.arange(GDN_TILE)[None, :]
    same = lambda size: (r // size) == (c // size)
    rows = [c <= r, c < r, c == r, same(INV_BASE)]
    rows += [same(2 * INV_BASE << i) & ~same(INV_BASE << i) for i in range(N_MERGE)]
    masks = jnp.asarray(np.stack(rows).astype(np.float32))
    tri = jnp.asarray(np.stack([c <= r, r <= c]).astype(np.float32)).astype(BF)
    return masks, tri


def _split3(x):
    h1 = x.astype(BF)
    r1 = x - h1.astype(F32)
    h2 = r1.astype(BF)
    return h1, h2, (r1 - h2.astype(F32)).astype(BF)


def _unit_lower_inverses(a, mask_ref):
    nk = [-(x * mask_ref[M_BASE]) for x in a]
    t = [mask_ref[M_EYE] + x for x in nk]
    for _ in range(int(np.log2(INV_BASE)) - 1):
        nk = [_dot(x, x) for x in nk]
        t = [x + _dot(x, y) for x, y in zip(t, nk)]
    for lvl in range(N_MERGE):
        et = [_dot(x * mask_ref[M_MERGE0 + lvl], y) for x, y in zip(a, t)]
        t = [x - _dot(x, y) for x, y in zip(t, et)]
    return t


def _gdn_kernel(qkv_ref, z_ref, sm_ref, cw_ref, arow_ref, acol_ref, drow_ref, dcol_ref, nw_ref, mask_ref, tri_ref,
                o_ref, carry_ref, state_ref):
    n = GDN_TILE
    heads = range(GDN_HEADS)

    @pl.when(pl.program_id(0) == 0)
    def _init():
        carry_ref[...] = jnp.zeros_like(carry_ref)
        state_ref[...] = jnp.zeros_like(state_ref)

    row8 = lax.broadcasted_iota(jnp.int32, (SUBLANES, LANES), 0)

    sm = sm_ref[...]
    g_cols = -jnp.exp(arow_ref[...]) * _softplus(sm + drow_ref[...])
    beta_cols = _sigmoid(sm)
    sm_t = sm.T
    ga = slice(SM_GA, SM_GA + GDN_HEADS)
    g_rows = -jnp.exp(acol_ref[ga, :]) * _softplus(sm_t[ga, :] + dcol_ref[ga, :])
    gc_cols = sum(jnp.dot(tri_ref[0], part, preferred_element_type=F32) for part in _split3(g_cols))
    gc_rows = sum(jnp.dot(part, tri_ref[1], preferred_element_type=F32) for part in _split3(g_rows))

    def conv_silu(off):
        x = qkv_ref[:, off:off + LANES]
        prev = carry_ref[:, off:off + LANES]
        w = cw_ref[:, off:off + LANES]
        acc = x * w[GDN_CONV - 1:GDN_CONV, :]
        for s in range(1, GDN_CONV):
            xs = pltpu.roll(x, s, axis=0)
            top = jnp.where(row8 < s, pltpu.roll(prev, s, axis=0), xs[0:SUBLANES, :])
            xs = jnp.concatenate([top, xs[SUBLANES:, :]], axis=0)
            acc = acc + xs * w[GDN_CONV - 1 - s:GDN_CONV - s, :]
        return acc * _sigmoid(acc)

    q = [conv_silu(h * HEAD_DIM) for h in heads]
    k = [conv_silu(GDN_WIDTH + h * HEAD_DIM) for h in heads]
    v = [conv_silu(2 * GDN_WIDTH + h * HEAD_DIM) for h in heads]
    q = [x * lax.rsqrt(jnp.sum(x * x, -1, keepdims=True) + RMS_EPS) * (HEAD_DIM ** -0.5) for x in q]
    k = [x * lax.rsqrt(jnp.sum(x * x, -1, keepdims=True) + RMS_EPS) for x in k]
    gcol = [gc_cols[:, SM_GA + h:SM_GA + h + 1] for h in heads]
    grow = [gc_rows[h:h + 1, :] for h in heads]
    bcol = [beta_cols[:, SM_GB + h:SM_GB + h + 1] for h in heads]
    glast = [x[n - 1:n, :] for x in gcol]
    ecol = [jnp.exp(x) for x in gcol]
    decay = [mask_ref[M_LOWER] * jnp.exp(jnp.minimum(c - r, 0.0)) for c, r in zip(gcol, grow)]
    k_t = [x.T for x in k]
    k_tb = [x.astype(BF) for x in k_t]
    kk = [jnp.dot(x.astype(BF), y, preferred_element_type=F32) for x, y in zip(k, k_tb)]
    qk = [jnp.dot(x.astype(BF), y, preferred_element_type=F32) for x, y in zip(q, k_tb)]
    a = [mask_ref[M_STRICT] * (b * x * d) for b, x, d in zip(bcol, kk, decay)]
    t = _unit_lower_inverses(a, mask_ref)
    rhs = [jnp.concatenate([v[h] * bcol[h], k[h] * (bcol[h] * ecol[h])], axis=1) for h in heads]
    sol = [_dot(x, y) for x, y in zip(t, rhs)]
    state = [state_ref[h] for h in heads]
    state_b = [x.astype(BF) for x in state]
    v_new = [x[:, :HEAD_DIM] - jnp.dot(x[:, HEAD_DIM:].astype(BF), s, preferred_element_type=F32)
             for x, s in zip(sol, state_b)]
    v_nb = [x.astype(BF) for x in v_new]
    o = [jnp.dot((q[h] * ecol[h]).astype(BF), state_b[h], preferred_element_type=F32)
         + jnp.dot((qk[h] * decay[h]).astype(BF), v_nb[h], preferred_element_type=F32) for h in heads]
    for h in heads:
        k_dec_t = k_t[h] * jnp.exp(glast[h] - grow[h])
        state_ref[h] = state[h] * jnp.exp(glast[h]) + jnp.dot(k_dec_t.astype(BF), v_nb[h],
                                                              preferred_element_type=F32)
    for h in heads:
        on = o[h] * lax.rsqrt(jnp.mean(o[h] * o[h], -1, keepdims=True) + RMS_EPS) * nw_ref[...]
        z = z_ref[:, h * HEAD_DIM:(h + 1) * HEAD_DIM]
        o_ref[:, h * HEAD_DIM:(h + 1) * HEAD_DIM] = (on * (z * _sigmoid(z))).astype(o_ref.dtype)

    carry_ref[...] = qkv_ref[n - SUBLANES:n, :]


def _gdn(pm, pt, conv_w, arow, acol, drow, dcol, norm_w, masks, tri, layer):
    s = pm.shape[0]
    n = GDN_TILE
    lsel3 = lambda i: (layer, 0, 0)
    const3 = lambda i: (0, 0, 0)
    return pl.pallas_call(
        _gdn_kernel,
        out_shape=jax.ShapeDtypeStruct((s, GDN_WIDTH), BF),
        grid=(s // n,),
        in_specs=[pl.BlockSpec((n, 3 * GDN_WIDTH), lambda i: (i, COL_GDN_QKV // (3 * GDN_WIDTH))),
                  pl.BlockSpec((n, GDN_WIDTH), lambda i: (i, COL_GDN_Z // GDN_WIDTH)),
                  pl.BlockSpec((n, LANES), lambda i: (i, COL_SMALL // LANES)),
                  pl.BlockSpec((None, GDN_CONV, 3 * GDN_WIDTH), lsel3),
                  pl.BlockSpec((None, 1, LANES), lsel3),
                  pl.BlockSpec((None, LANES, 1), lsel3),
                  pl.BlockSpec((None, 1, LANES), lsel3),
                  pl.BlockSpec((None, LANES, 1), lsel3),
                  pl.BlockSpec((None, 1, HEAD_DIM), lsel3),
                  pl.BlockSpec(masks.shape, const3, pipeline_mode=pl.Buffered(1)),
                  pl.BlockSpec(tri.shape, const3, pipeline_mode=pl.Buffered(1))],
        out_specs=pl.BlockSpec((n, GDN_WIDTH), lambda i: (i, 0)),
        scratch_shapes=[pltpu.VMEM((SUBLANES, 3 * GDN_WIDTH), F32),
                        pltpu.VMEM((GDN_HEADS, HEAD_DIM, HEAD_DIM), F32)],
        compiler_params=_cparams("arbitrary"),
        name="gdn",
    )(pm, pm, pt, conv_w, arow, acol, drow, dcol, norm_w, masks, tri)


def _dsa_prep_kernel(aq_ref, iq_ref, ckv_ref, sm_ref, wk_ref, wvt_ref, kvg_ref, ig_ref, ib_ref,
                     cos_ref, sins_ref, cos_t_ref, sin_t_ref, icos_t_ref, isin_t_ref, icos_ref, isin_ref,
                     qt_ref, k_ref, vt_ref, qit_ref, ki_ref, wt_ref):
    half = HEAD_DIM // 2
    ihalf = IDX_DIM // 2

    aq_t = aq_ref[...].T
    cos_t = cos_t_ref[...]
    sin_t = sin_t_ref[...]
    for h in range(ATT_HEADS):
        x = aq_t[h * HEAD_DIM:(h + 1) * HEAD_DIM, :]
        rot = jnp.concatenate([-x[half:, :], x[:half, :]], axis=0)
        qt_ref[h] = ((x * cos_t + rot * sin_t) * (LOG2E * HEAD_DIM ** -0.5)).astype(qt_ref.dtype)

    iq_t = iq_ref[...].T
    icos_t = icos_t_ref[...]
    isin_t = isin_t_ref[...]
    for h in range(IDX_HEADS):
        x = iq_t[h * IDX_DIM:(h + 1) * IDX_DIM, :]
        rot = jnp.concatenate([-x[ihalf:, :], x[:ihalf, :]], axis=0)
        qit_ref[h * IDX_DIM:(h + 1) * IDX_DIM, :] = (
            (x * icos_t + rot * isin_t) * (IDX_DIM ** -0.5)).astype(qit_ref.dtype)

    c = ckv_ref[...]
    kvn = c * lax.rsqrt(jnp.mean(c * c, -1, keepdims=True) + RMS_EPS) * kvg_ref[...]
    k = jnp.dot(kvn.astype(BF), wk_ref[...], preferred_element_type=F32)
    cos = cos_ref[...]
    sins = sins_ref[...]
    for h in range(ATT_HEADS):
        x = k[:, h * HEAD_DIM:(h + 1) * HEAD_DIM]
        k_ref[h] = (x * cos + pltpu.roll(x, half, axis=1) * sins).astype(k_ref.dtype)
    v_t = jnp.dot(wvt_ref[...], kvn.T.astype(BF), preferred_element_type=F32)
    for h in range(ATT_HEADS):
        vt_ref[h, 0, :HEAD_DIM, :] = v_t[h * HEAD_DIM:(h + 1) * HEAD_DIM, :].astype(vt_ref.dtype)
        vt_ref[h, 0, HEAD_DIM:, :] = jnp.ones((V_ROWS - HEAD_DIM, v_t.shape[1]), vt_ref.dtype)

    sm = sm_ref[...]
    lane = lax.broadcasted_iota(jnp.int32, sm.shape, 1)
    is_k = lane < IDX_DIM
    mu = jnp.sum(jnp.where(is_k, sm, 0.0), -1, keepdims=True) * (1.0 / IDX_DIM)
    d = jnp.where(is_k, sm - mu, 0.0)
    var = jnp.sum(d * d, -1, keepdims=True) * (1.0 / IDX_DIM)
    kin = d * lax.rsqrt(var + LN_EPS) * ig_ref[...] + ib_ref[...]
    below = pltpu.roll(kin, ihalf, axis=1)
    above = pltpu.roll(kin, LANES - ihalf, axis=1)
    rot = jnp.where((lane & (IDX_DIM - 1)) < ihalf, -above, below)
    kir = kin * icos_ref[...] + rot * isin_ref[...]
    ki_ref[...] = kir[:, :IDX_DIM].astype(ki_ref.dtype)
    wt_ref[...] = sm.T[SM_IW:SM_IW + IDX_HEADS, :] * (IDX_HEADS ** -0.5)


def _dsa_prep(p, wk, wvt, kvg, ig, ib, tabs, layer):
    s = p.shape[0]
    tp = KC
    lsel3 = lambda i: (layer, 0, 0)
    row = lambda i: (i, 0)
    col = lambda i: (0, i)
    out_shape = (jax.ShapeDtypeStruct((ATT_HEADS, HEAD_DIM, s), BF),
                 jax.ShapeDtypeStruct((ATT_HEADS, s, HEAD_DIM), BF),
                 jax.ShapeDtypeStruct((ATT_HEADS, s // tp, V_ROWS, tp), BF),
                 jax.ShapeDtypeStruct((IDX_HEADS * IDX_DIM, s), BF),
                 jax.ShapeDtypeStruct((s, IDX_DIM), BF),
                 jax.ShapeDtypeStruct((IDX_HEADS, s), F32))
    return pl.pallas_call(
        _dsa_prep_kernel,
        out_shape=out_shape,
        grid=(s // tp,),
        in_specs=[pl.BlockSpec((tp, ATT_WIDTH), lambda i: (i, COL_ATT_Q // ATT_WIDTH)),
                  pl.BlockSpec((tp, IDX_HEADS * IDX_DIM), lambda i: (i, COL_IDX_Q // (IDX_HEADS * IDX_DIM))),
                  pl.BlockSpec((tp, KV_RANK), lambda i: (i, COL_KV // KV_RANK)),
                  pl.BlockSpec((tp, LANES), lambda i: (i, COL_SMALL // LANES)),
                  pl.BlockSpec((None, KV_RANK, ATT_WIDTH), lsel3),
                  pl.BlockSpec((None, ATT_WIDTH, KV_RANK), lsel3),
                  pl.BlockSpec((None, 1, KV_RANK), lsel3),
                  pl.BlockSpec((None, 1, LANES), lsel3),
                  pl.BlockSpec((None, 1, LANES), lsel3),
                  pl.BlockSpec((tp, HEAD_DIM), row), pl.BlockSpec((tp, HEAD_DIM), row),
                  pl.BlockSpec((HEAD_DIM, tp), col), pl.BlockSpec((HEAD_DIM, tp), col),
                  pl.BlockSpec((IDX_DIM, tp), col), pl.BlockSpec((IDX_DIM, tp), col),
                  pl.BlockSpec((tp, LANES), row), pl.BlockSpec((tp, LANES), row)],
        out_specs=(pl.BlockSpec((ATT_HEADS, HEAD_DIM, tp), lambda i: (0, 0, i)),
                   pl.BlockSpec((ATT_HEADS, tp, HEAD_DIM), lambda i: (0, i, 0)),
                   pl.BlockSpec((ATT_HEADS, 1, V_ROWS, tp), lambda i: (0, i, 0, 0)),
                   pl.BlockSpec((IDX_HEADS * IDX_DIM, tp), col),
                   pl.BlockSpec((tp, IDX_DIM), row),
                   pl.BlockSpec((IDX_HEADS, tp), col)),
        compiler_params=_cparams("parallel"),
        name="dsa_prep",
    )(p, p, p, p, wk, wvt, kvg, ig, ib, *tabs)


def _dsa_kernel(qit_ref, wt_ref, ki_ref, qt_ref, k_ref, vt_ref, o_ref, sc_ref, sh_ref, m_ref, al_ref, acc_ref,
                s_ref, *, top_k):
    i = pl.program_id(0)
    n_chunks = i + 1
    n_pairs = lax.shift_right_logical(n_chunks, 1)

    def chunk(c):
        return pl.ds(pl.multiple_of(c * KC, KC), KC)

    def pair(c2):
        return pl.ds(pl.multiple_of(c2 * (2 * KC), 2 * KC), 2 * KC)

    def high_half(v):
        return lax.bitcast_convert_type(lax.bitcast_convert_type(v, jnp.int32) & jnp.int32(-2 ** 16), F32)

    w_t = wt_ref[...]

    def score_rows(rows, n_rows, first_key):
        kic = ki_ref[rows, :]
        acc = jnp.zeros((n_rows, TQ), F32)
        for h in range(IDX_HEADS):
            d = jnp.dot(kic, qit_ref[h * IDX_DIM:(h + 1) * IDX_DIM, :], preferred_element_type=F32)
            acc = acc + w_t[h:h + 1, :] * jnp.maximum(d, 0.0)
        visible = (first_key + lax.broadcasted_iota(jnp.int32, (n_rows, TQ), 0)
                   <= i * TQ + lax.broadcasted_iota(jnp.int32, (n_rows, TQ), 1))
        score = jnp.where(visible, acc, jnp.nan)
        sc_ref[rows, :] = score
        sh_ref[rows, :] = high_half(score).astype(sh_ref.dtype)

    @pl.loop(0, n_pairs)
    def _score_pairs(c2):
        score_rows(pair(c2), 2 * KC, c2 * (2 * KC))

    @pl.loop(2 * n_pairs, n_chunks)
    def _score_rest(c):
        score_rows(chunk(c), KC, c * KC)

    def count(pred):
        def part(rows):
            one = jnp.where(pred(sc_ref[rows, :]), 1.0, 0.0)
            return jnp.sum(one.reshape(-1, COUNT_ROWS, TQ), axis=0)

        cnt = lax.fori_loop(0, n_pairs, lambda c2, cnt: cnt + part(pair(c2)), jnp.zeros((COUNT_ROWS, TQ), F32))
        cnt = lax.fori_loop(2 * n_pairs, n_chunks, lambda c, cnt: cnt + part(chunk(c)), cnt)
        return jnp.sum(cnt, axis=0, keepdims=True)

    def count_high(thr_h):
        def part(rows):
            one = jnp.where(sh_ref[rows, :] >= thr_h, jnp.ones((), BF), jnp.zeros((), BF))
            one = one.reshape(-1, COUNT_ROWS, TQ)
            tot = one[0]
            for g in range(1, one.shape[0]):
                tot = tot + one[g]
            return tot

        cnt = lax.fori_loop(0, n_pairs, lambda c2, cnt: cnt + part(pair(c2)), jnp.zeros((COUNT_ROWS, TQ), BF))
        cnt = lax.fori_loop(2 * n_pairs, n_chunks, lambda c, cnt: cnt + part(chunk(c)), cnt)
        return jnp.sum(cnt.astype(F32), axis=0, keepdims=True)

    def key_to_f32(cu):
        ks = cu ^ jnp.int32(-2 ** 31)
        bits = jnp.where(ks < 0, ks ^ jnp.int32(2 ** 31 - 1), ks)
        return lax.bitcast_convert_type(bits, F32)

    def bit_pass(it, st, high=False):
        cu, done = st
        cand = cu | lax.shift_left(jnp.int32(1), 31 - it)
        thr_c = key_to_f32(cand)
        cnt = count_high(high_half(thr_c).astype(BF)) if high else count(lambda x: x >= thr_c)
        take = jnp.logical_and(cnt >= top_k, done == 0)
        return jnp.where(take, cand, cu), jnp.where(jnp.logical_and(take, cnt == top_k), 1, done)

    def bits_left(st):
        it, _, done = st
        return jnp.logical_and(it < 32, jnp.min(done) == 0)

    def two_passes(st):
        it, cu, done = st
        cu, done = lax.fori_loop(it, it + 2, bit_pass, (cu, done))
        return it + 2, cu, done

    st = lax.fori_loop(0, 16, functools.partial(bit_pass, high=True),
                       (jnp.zeros((1, TQ), jnp.int32), jnp.zeros((1, TQ), jnp.int32)))
    st = lax.fori_loop(16, UNTESTED_BITS, bit_pass, st)
    _, cu, _ = lax.while_loop(bits_left, two_passes, (jnp.int32(UNTESTED_BITS),) + st)
    thr = jnp.where((cu & jnp.int32(-2 ** 23)) == 0, -jnp.inf, key_to_f32(cu))
    n_ge = count(lambda x: x >= thr)
    has_tie = jnp.max(n_ge) > top_k

    @pl.when(jnp.logical_not(has_tie))
    def _select():
        def body(c, carry):
            sc_ref[chunk(c), :] = jnp.where(sc_ref[chunk(c), :] >= thr, 0.0, NEG_BIAS)
            return carry
        lax.fori_loop(0, n_chunks, body, 0)

    @pl.when(has_tie)
    def _select_ties():
        need = top_k - count(lambda x: x > thr)
        ltri = jnp.where(lax.broadcasted_iota(jnp.int32, (KC, KC), 1)
                         <= lax.broadcasted_iota(jnp.int32, (KC, KC), 0), 1.0, 0.0).astype(BF)

        def body(c, seen):
            x = sc_ref[chunk(c), :]
            eq = x == thr
            rank = jnp.dot(ltri, jnp.where(eq, 1.0, 0.0).astype(BF), preferred_element_type=F32) + seen
            keep = jnp.logical_or(x > thr, jnp.logical_and(eq, rank <= need))
            sc_ref[chunk(c), :] = jnp.where(keep, 0.0, NEG_BIAS)
            return rank[KC - 1:KC, :]
        lax.fori_loop(0, n_chunks, body, jnp.zeros((1, TQ), F32))

    m_ref[...] = jnp.full(m_ref.shape, -jnp.inf, F32)
    acc_ref[...] = jnp.zeros_like(acc_ref)

    def attend(rows, n_sub, first_chunk):
        n_rows = n_sub * KC
        for h in range(ATT_HEADS):
            s = jnp.dot(k_ref[h, rows, :], qt_ref[h], preferred_element_type=F32) + sc_ref[rows, :]
            s_ref[h, :n_rows, :] = s
            m_old = m_ref[h:h + 1, :]
            m_new = jnp.maximum(m_old, jnp.max(s, axis=0, keepdims=True))
            al_ref[h:h + 1, :] = jnp.exp2(m_old - m_new)
            m_ref[h:h + 1, :] = m_new
        for h in range(ATT_HEADS):
            p = jnp.exp2(s_ref[h, :n_rows, :] - m_ref[h:h + 1, :]).astype(BF)
            pv = jnp.dot(vt_ref[h, first_chunk], p[:KC, :], preferred_element_type=F32)
            for j in range(1, n_sub):
                pv = pv + jnp.dot(vt_ref[h, first_chunk + j], p[j * KC:(j + 1) * KC, :],
                                  preferred_element_type=F32)
            acc_ref[h] = al_ref[h:h + 1, :] * acc_ref[h] + pv

    @pl.loop(0, n_pairs)
    def _attend_pairs(c2):
        attend(pair(c2), 2, 2 * c2)

    @pl.loop(2 * n_pairs, n_chunks)
    def _attend_rest(c):
        attend(chunk(c), 1, c)
    for h in range(ATT_HEADS):
        o_t = acc_ref[h, :HEAD_DIM, :] / acc_ref[h, HEAD_DIM:HEAD_DIM + 1, :]
        o_ref[:, h * HEAD_DIM:(h + 1) * HEAD_DIM] = o_t.T.astype(o_ref.dtype)


def _dsa(qt, k3, vt4, qit, ki, wt):
    s = ki.shape[0]
    top_k = min(INDEX_TOPK, s // 4)
    resident = pl.Buffered(1)
    return pl.pallas_call(
        functools.partial(_dsa_kernel, top_k=top_k),
        out_shape=jax.ShapeDtypeStruct((s, ATT_WIDTH), BF),
        grid=(s // TQ,),
        in_specs=[pl.BlockSpec((IDX_HEADS * IDX_DIM, TQ), lambda i: (0, i)),
                  pl.BlockSpec((IDX_HEADS, TQ), lambda i: (0, i)),
                  pl.BlockSpec((s, IDX_DIM), lambda i: (0, 0), pipeline_mode=resident),
                  pl.BlockSpec((ATT_HEADS, HEAD_DIM, TQ), lambda i: (0, 0, i)),
                  pl.BlockSpec((ATT_HEADS, s, HEAD_DIM), lambda i: (0, 0, 0), pipeline_mode=resident),
                  pl.BlockSpec((ATT_HEADS, s // KC, V_ROWS, KC), lambda i: (0, 0, 0, 0),
                               pipeline_mode=resident)],
        out_specs=pl.BlockSpec((TQ, ATT_WIDTH), lambda i: (i, 0)),
        scratch_shapes=[pltpu.VMEM((s, TQ), F32),
                        pltpu.VMEM((s, TQ), BF),
                        pltpu.VMEM((ATT_HEADS, TQ), F32),
                        pltpu.VMEM((ATT_HEADS, TQ), F32),
                        pltpu.VMEM((ATT_HEADS, V_ROWS, TQ), F32),
                        pltpu.VMEM((ATT_HEADS, 2 * KC, TQ), F32)],
        compiler_params=_cparams("arbitrary"),
        name="dsa",
    )(qit, wt, ki, qt, k3, vt4)


def _layer_norm(r, g, b):
    mu = jnp.mean(r, -1, keepdims=True)
    d = r - mu
    var = jnp.mean(d * d, -1, keepdims=True)
    return d * lax.rsqrt(var + LN_EPS) * g + b


def _out_ln_kernel(a1_ref, a2_ref, w1_ref, w2_ref, x_ref, g_ref, b_ref, xo_ref, xb_ref):
    y = (jnp.dot(a1_ref[...], w1_ref[...], preferred_element_type=F32)
         + jnp.dot(a2_ref[...], w2_ref[...], preferred_element_type=F32))
    o = _layer_norm(DN_ALPHA * x_ref[...] + y, g_ref[...], b_ref[...])
    xo_ref[...] = o
    xb_ref[...] = o.astype(xb_ref.dtype)


def _out_ln(o_gdn, o_dsa, w_out, xf, g, b, layer):
    s = xf.shape[0]
    tm = min(512, s)
    row = lambda i: (i, 0)
    lsel3 = lambda i: (layer, 0, 0)
    return pl.pallas_call(
        _out_ln_kernel,
        out_shape=(jax.ShapeDtypeStruct((s, D_MODEL), F32), jax.ShapeDtypeStruct((s, D_MODEL), BF)),
        grid=(s // tm,),
        in_specs=[pl.BlockSpec((tm, GDN_WIDTH), row), pl.BlockSpec((tm, ATT_WIDTH), row),
                  pl.BlockSpec((None, GDN_WIDTH, D_MODEL), lsel3, pipeline_mode=pl.Buffered(1)),
                  pl.BlockSpec((None, ATT_WIDTH, D_MODEL), lambda i: (layer, 1, 0),
                               pipeline_mode=pl.Buffered(1)),
                  pl.BlockSpec((tm, D_MODEL), row),
                  pl.BlockSpec((None, 1, D_MODEL), lsel3), pl.BlockSpec((None, 1, D_MODEL), lsel3)],
        out_specs=(pl.BlockSpec((tm, D_MODEL), row), pl.BlockSpec((tm, D_MODEL), row)),
        compiler_params=_cparams("parallel"),
        name="out_ln",
    )(o_gdn, o_dsa, w_out, w_out, xf, g, b)


def _ffn_up_kernel(x_ref, wg_ref, wv_ref, cg_ref, cv_ref, bg_ref, bv_ref, h_ref, carry_ref, wb_ref):
    tm, tn = h_ref.shape

    @pl.when(pl.program_id(1) == 0)
    def _init():
        carry_ref[...] = jnp.zeros_like(carry_ref)
        wb_ref[0] = wg_ref[...].astype(wb_ref.dtype)
        wb_ref[1] = wv_ref[...].astype(wb_ref.dtype)

    row8 = lax.broadcasted_iota(jnp.int32, (SUBLANES, h_ref.shape[1]), 0)

    def conv(u, prev, w, b):
        acc = u * w[FFN_CONV - 1:FFN_CONV, :] + b
        for s in range(1, FFN_CONV):
            us = pltpu.roll(u, s, axis=0)
            top = jnp.where(row8 < s, pltpu.roll(prev, s, axis=0), us[0:SUBLANES, :])
            us = jnp.concatenate([top, us[SUBLANES:, :]], axis=0)
            acc = acc + us * w[FFN_CONV - 1 - s:FFN_CONV - s, :]
        return acc

    x = x_ref[...]
    ug = jnp.dot(x, wb_ref[0], preferred_element_type=F32)
    uv = jnp.dot(x, wb_ref[1], preferred_element_type=F32)
    gate = conv(ug, carry_ref[0], cg_ref[...], bg_ref[...])
    val = conv(uv, carry_ref[1], cv_ref[...], bv_ref[...])
    carry_ref[0] = ug[tm - SUBLANES:, :]
    carry_ref[1] = uv[tm - SUBLANES:, :]
    h_ref[...] = (gate * _sigmoid(gate) * val).astype(h_ref.dtype)


def _ffn_up(xb, w_up, conv_w, conv_b, layer):
    s = xb.shape[0]
    tm = min(1024, s)
    tn = 512
    nj = D_FF // tn
    return pl.pallas_call(
        _ffn_up_kernel,
        out_shape=jax.ShapeDtypeStruct((s, D_FF), BF),
        grid=(nj, s // tm),
        in_specs=[pl.BlockSpec((tm, D_MODEL), lambda j, i: (i, 0)),
                  pl.BlockSpec((None, D_MODEL, tn), lambda j, i: (layer, 0, j)),
                  pl.BlockSpec((None, D_MODEL, tn), lambda j, i: (layer, 0, j + nj)),
                  pl.BlockSpec((None, FFN_CONV, tn), lambda j, i: (layer, 0, j)),
                  pl.BlockSpec((None, FFN_CONV, tn), lambda j, i: (layer, 0, j + nj)),
                  pl.BlockSpec((None, 1, tn), lambda j, i: (layer, 0, j)),
                  pl.BlockSpec((None, 1, tn), lambda j, i: (layer, 0, j + nj))],
        out_specs=pl.BlockSpec((tm, tn), lambda j, i: (i, j)),
        scratch_shapes=[pltpu.VMEM((2, SUBLANES, tn), F32),
                        pltpu.VMEM((2, D_MODEL, tn), BF)],
        compiler_params=_cparams("arbitrary", "arbitrary"),
        name="ffn_up",
    )(xb, w_up, w_up, conv_w, conv_w, conv_b, conv_b)


def _ffn_down_kernel(h_ref, w_ref, x_ref, g_ref, b_ref, xo_ref, xb_ref):
    f = jnp.dot(h_ref[...], w_ref[...], preferred_element_type=F32)
    o = _layer_norm(DN_ALPHA * x_ref[...] + f, g_ref[...], b_ref[...])
    xo_ref[...] = o
    xb_ref[...] = o.astype(xb_ref.dtype)


def _ffn_down_ln(h, w_down, xf, g, b, layer):
    s = xf.shape[0]
    tm = min(256, s)
    row = lambda i: (i, 0)
    lsel3 = lambda i: (layer, 0, 0)
    return pl.pallas_call(
        _ffn_down_kernel,
        out_shape=(jax.ShapeDtypeStruct((s, D_MODEL), F32), jax.ShapeDtypeStruct((s, D_MODEL), BF)),
        grid=(s // tm,),
        in_specs=[pl.BlockSpec((tm, D_FF), row),
                  pl.BlockSpec((None, D_FF, D_MODEL), lsel3, pipeline_mode=pl.Buffered(1)),
                  pl.BlockSpec((tm, D_MODEL), row),
                  pl.BlockSpec((None, 1, D_MODEL), lsel3), pl.BlockSpec((None, 1, D_MODEL), lsel3)],
        out_specs=(pl.BlockSpec((tm, D_MODEL), row), pl.BlockSpec((tm, D_MODEL), row)),
        compiler_params=_cparams("parallel"),
        name="ffn_down",
    )(h, w_down, xf, g, b)


def _rope_tables(seq, dim):
    inv = ROPE_THETA ** (-jnp.arange(0, dim, 2, dtype=F32) / dim)
    ang = jnp.arange(seq, dtype=F32)[:, None] * inv[None, :]
    ang = jnp.concatenate([ang, ang], -1)
    return jnp.cos(ang), jnp.sin(ang)


def _lane_pad(v, offset):
    out = jnp.zeros((v.shape[0], 1, LANES), F32)
    return out.at[:, 0, offset:offset + v.shape[1]].set(v.astype(F32))


def kernel(x, w_in, gdn_conv_w, gdn_a_log, gdn_dt_bias, gdn_norm_w, kv_norm_w, w_ukv, idx_k_norm_g,
           idx_k_norm_b, w_out, ln1_g, ln1_b, ffn_up, ffn_conv_w, ffn_conv_b, ffn_down, ln2_g, ln2_b):
    seq = x.shape[1]
    depth = w_in.shape[0]

    o_ga = MAIN_WIDTH
    o_aq = o_ga + 2 * GDN_HEADS
    o_ckv = o_aq + ATT_WIDTH
    o_iq = o_ckv + KV_RANK
    o_ik = o_iq + IDX_HEADS * IDX_DIM
    o_iw = o_ik + IDX_DIM
    o_end = o_iw + IDX_HEADS
    w_main = w_in[:, :, :o_ga].astype(BF)
    w_rest = w_in[:, :, o_ga:].astype(BF)
    cut = lambda lo, hi: w_rest[:, :, lo - o_ga:hi - o_ga]
    w_tail = jnp.concatenate(
        [cut(o_aq, o_ckv), cut(o_iq, o_ik), cut(o_ckv, o_iq), cut(o_ik, o_end), cut(o_ga, o_aq),
         jnp.zeros((depth, D_MODEL, TAIL_WIDTH - (o_end - o_ga)), BF)], axis=-1)
    wk = w_ukv[:, :, :ATT_WIDTH].astype(BF)
    wvt = jnp.swapaxes(w_ukv[:, :, ATT_WIDTH:], 1, 2).astype(BF)
    w_out_b = w_out.astype(BF)
    w_down_b = ffn_down.astype(BF)

    arow = _lane_pad(gdn_a_log, SM_GA)
    drow = _lane_pad(gdn_dt_bias, SM_GA)
    acol = jnp.swapaxes(arow, 1, 2)
    dcol = jnp.swapaxes(drow, 1, 2)
    ig = _lane_pad(idx_k_norm_g, 0)
    ib = _lane_pad(idx_k_norm_b, 0)
    r3 = lambda v: v.astype(F32)[:, None, :]

    cos, sin = _rope_tables(seq, HEAD_DIM)
    icos, isin = _rope_tables(seq, IDX_DIM)
    sign = jnp.where(jnp.arange(HEAD_DIM) < HEAD_DIM // 2, -1.0, 1.0).astype(F32)
    lane_fill = lambda t: jnp.concatenate([t, jnp.zeros_like(t)], axis=-1)
    tabs = (cos, sin * sign, cos.T, sin.T, icos.T, isin.T, lane_fill(icos), lane_fill(isin))

    masks, tri = _gdn_masks()

    xf = x[0]
    xb = xf.astype(BF)
    for layer in range(depth):
        pm = _proj_main(xb, w_main, layer)
        pt = _proj_tail(xb, w_tail, layer)
        o_gdn = _gdn(pm, pt, gdn_conv_w, arow, acol, drow, dcol, r3(gdn_norm_w), masks, tri, layer)
        qt, k3, vt4, qit, ki, wt = _dsa_prep(pt, wk, wvt, r3(kv_norm_w), ig, ib, tabs, layer)
        o_dsa = _dsa(qt, k3, vt4, qit, ki, wt)
        xf, xb = _out_ln(o_gdn, o_dsa, w_out_b, xf, r3(ln1_g), r3(ln1_b), layer)
        h = _ffn_up(xb, ffn_up, ffn_conv_w, r3(ffn_conv_b), layer)
        xf, xb = _ffn_down_ln(h, w_down_b, xf, r3(ln2_g), r3(ln2_b), layer)
    return xf[None]
```

```python
import functools

import jax
import jax.numpy as jnp
import numpy as np
from jax import lax
from jax.experimental import pallas as pl
from jax.experimental.pallas import tpu as pltpu

D_MODEL = 2048
DEPTH = 4
HEAD_DIM = 128
GDN_HEADS = 8
ATT_HEADS = 8
GDN_WIDTH = GDN_HEADS * HEAD_DIM
ATT_WIDTH = ATT_HEADS * HEAD_DIM
GDN_CONV = 4
KV_RANK = 256
IDX_HEADS = 16
IDX_DIM = 64
INDEX_TOPK = 256
ROPE_THETA = 10000.0
D_FF = 5632
FFN_CONV = 3
LN_EPS = 1e-5
RMS_EPS = 1e-6
DN_ALPHA = (2 * DEPTH) ** 0.25

BF = jnp.bfloat16
F32 = jnp.float32
HIGHEST = lax.Precision.HIGHEST

LANES = 128
SUBLANES = 8
VMEM_LIMIT_BYTES = 58 * 1024 * 1024

MAIN_WIDTH = 4 * GDN_WIDTH
COL_GDN_QKV = 0
COL_GDN_Z = 3 * GDN_WIDTH
COL_ATT_Q = 0
COL_IDX_Q = COL_ATT_Q + ATT_WIDTH
COL_KV = COL_IDX_Q + IDX_HEADS * IDX_DIM
COL_SMALL = COL_KV + KV_RANK
SM_IW = IDX_DIM
SM_GA = SM_IW + IDX_HEADS
SM_GB = SM_GA + GDN_HEADS
TAIL_WIDTH = COL_SMALL + 2 * LANES

GDN_TILE = 256
INV_BASE = 16
GDN_GROUP = 4
TQ = 256
KC = 256
COUNT_ROWS = 64
UNTESTED_BITS = 22
NEG_BIAS = -1e30
V_ROWS = HEAD_DIM + 16
LOG2E = 1.4426950408889634


def _cparams(*sem):
    return pltpu.CompilerParams(dimension_semantics=sem, vmem_limit_bytes=VMEM_LIMIT_BYTES)


def _sigmoid(x):
    return 1.0 / (1.0 + jnp.exp(-x))


def _softplus(x):
    return jnp.maximum(x, 0.0) + jnp.log(1.0 + jnp.exp(-jnp.abs(x)))


def _dot(a, b):
    return jnp.dot(a.astype(BF), b.astype(BF), preferred_element_type=F32)


def _mm_kernel(a_ref, b_ref, o_ref):
    o_ref[...] = jnp.dot(a_ref[...], b_ref[...], preferred_element_type=F32)


def _proj_main(xb, w_main, layer):
    s, k = xb.shape
    tm = min(1024, s)
    tn = 1024
    return pl.pallas_call(
        _mm_kernel,
        out_shape=jax.ShapeDtypeStruct((s, MAIN_WIDTH), F32),
        grid=(MAIN_WIDTH // tn, s // tm),
        in_specs=[pl.BlockSpec((tm, k), lambda j, i: (i, 0)),
                  pl.BlockSpec((None, k, tn), lambda j, i: (layer, 0, j))],
        out_specs=pl.BlockSpec((tm, tn), lambda j, i: (i, j)),
        compiler_params=_cparams("parallel", "parallel"),
        name="proj_main",
    )(xb, w_main)


def _proj_tail(xb, w_tail, layer):
    s, k = xb.shape
    tm = min(1024, s)
    tn = TAIL_WIDTH // 2
    return pl.pallas_call(
        _mm_kernel,
        out_shape=jax.ShapeDtypeStruct((s, TAIL_WIDTH), F32),
        grid=(TAIL_WIDTH // tn, s // tm),
        in_specs=[pl.BlockSpec((tm, k), lambda j, i: (i, 0)),
                  pl.BlockSpec((None, k, tn), lambda j, i: (layer, 0, j))],
        out_specs=pl.BlockSpec((tm, tn), lambda j, i: (i, j)),
        compiler_params=_cparams("parallel", "parallel"),
        name="proj_tail",
    )(xb, w_tail)


M_LOWER, M_STRICT, M_EYE, M_BASE, M_MERGE0 = 0, 1, 2, 3, 4
N_MERGE = int(np.log2(GDN_TILE // INV_BASE))


def _gdn_masks():
    r = np.arange(GDN_TILE)[:, None]
    c = np.arange(GDN_TILE)[None, :]
    same = lambda size: (r // size) == (c // size)
    rows = [c <= r, c < r, c == r, same(INV_BASE)]
    rows += [same(2 * INV_BASE << i) & ~same(INV_BASE << i) for i in range(N_MERGE)]
    masks = jnp.asarray(np.stack(rows).astype(np.float32))
    tri = jnp.asarray(np.stack([c <= r, r <= c]).astype(np.float32)).astype(BF)
    return masks, tri


def _split3(x):
    h1 = x.astype(BF)
    r1 = x - h1.astype(F32)
    h2 = r1.astype(BF)
    return h1, h2, (r1 - h2.astype(F32)).astype(BF)


def _unit_lower_inverses(a, mask_ref):
    nk = [-(x * mask_ref[M_BASE]) for x in a]
    t = [mask_ref[M_EYE] + x for x in nk]
    for _ in range(int(np.log2(INV_BASE)) - 1):
        nk = [_dot(x, x) for x in nk]
        t = [x + _dot(x, y) for x, y in zip(t, nk)]
    for lvl in range(N_MERGE):
        et = [_dot(x * mask_ref[M_MERGE0 + lvl], y) for x, y in zip(a, t)]
        t = [x - _dot(x, y) for x, y in zip(t, et)]
    return t


def _gdn_kernel(qkv_ref, z_ref, sm_ref, cw_ref, arow_ref, acol_ref, drow_ref, dcol_ref, nw_ref, mask_ref, tri_ref,
                o_ref, carry_ref, state_ref):
    n = GDN_TILE

    @pl.when(pl.program_id(0) == 0)
    def _init():
        carry_ref[...] = jnp.zeros_like(carry_ref)
        state_ref[...] = jnp.zeros_like(state_ref)

    row8 = lax.broadcasted_iota(jnp.int32, (SUBLANES, LANES), 0)

    sm = sm_ref[...]
    g_cols = -jnp.exp(arow_ref[...]) * _softplus(sm + drow_ref[...])
    beta_cols = _sigmoid(sm)
    sm_t = sm.T
    ga = slice(SM_GA, SM_GA + GDN_HEADS)
    g_rows = -jnp.exp(acol_ref[ga, :]) * _softplus(sm_t[ga, :] + dcol_ref[ga, :])
    gc_cols = sum(jnp.dot(tri_ref[0], part, preferred_element_type=F32) for part in _split3(g_cols))
    gc_rows = sum(jnp.dot(part, tri_ref[1], preferred_element_type=F32) for part in _split3(g_rows))

    def conv_silu(off):
        x = qkv_ref[:, off:off + LANES]
        prev = carry_ref[:, off:off + LANES]
        w = cw_ref[:, off:off + LANES]
        acc = x * w[GDN_CONV - 1:GDN_CONV, :]
        for s in range(1, GDN_CONV):
            xs = pltpu.roll(x, s, axis=0)
            top = jnp.where(row8 < s, pltpu.roll(prev, s, axis=0), xs[0:SUBLANES, :])
            xs = jnp.concatenate([top, xs[SUBLANES:, :]], axis=0)
            acc = acc + xs * w[GDN_CONV - 1 - s:GDN_CONV - s, :]
        return acc * _sigmoid(acc)

    def run_heads(hs):
        idx = range(len(hs))
        q = [conv_silu(h * HEAD_DIM) for h in hs]
        k = [conv_silu(GDN_WIDTH + h * HEAD_DIM) for h in hs]
        v = [conv_silu(2 * GDN_WIDTH + h * HEAD_DIM) for h in hs]
        q = [x * lax.rsqrt(jnp.sum(x * x, -1, keepdims=True) + RMS_EPS) * (HEAD_DIM ** -0.5) for x in q]
        k = [x * lax.rsqrt(jnp.sum(x * x, -1, keepdims=True) + RMS_EPS) for x in k]
        gcol = [gc_cols[:, SM_GA + h:SM_GA + h + 1] for h in hs]
        grow = [gc_rows[h:h + 1, :] for h in hs]
        bcol = [beta_cols[:, SM_GB + h:SM_GB + h + 1] for h in hs]
        glast = [x[n - 1:n, :] for x in gcol]
        ecol = [jnp.exp(x) for x in gcol]
        decay = [mask_ref[M_LOWER] * jnp.exp(jnp.minimum(c - r, 0.0)) for c, r in zip(gcol, grow)]
        k_t = [x.T for x in k]
        k_tb = [x.astype(BF) for x in k_t]
        kk = [jnp.dot(x.astype(BF), y, preferred_element_type=F32) for x, y in zip(k, k_tb)]
        qk = [jnp.dot(x.astype(BF), y, preferred_element_type=F32) for x, y in zip(q, k_tb)]
        a = [mask_ref[M_STRICT] * (b * x * d) for b, x, d in zip(bcol, kk, decay)]
        t = _unit_lower_inverses(a, mask_ref)
        rhs = [jnp.concatenate([v[i] * bcol[i], k[i] * (bcol[i] * ecol[i])], axis=1) for i in idx]
        sol = [_dot(x, y) for x, y in zip(t, rhs)]
        state = [state_ref[h] for h in hs]
        state_b = [x.astype(BF) for x in state]
        v_new = [x[:, :HEAD_DIM] - jnp.dot(x[:, HEAD_DIM:].astype(BF), s, preferred_element_type=F32)
                 for x, s in zip(sol, state_b)]
        v_nb = [x.astype(BF) for x in v_new]
        o = [jnp.dot((q[i] * ecol[i]).astype(BF), state_b[i], preferred_element_type=F32)
             + jnp.dot((qk[i] * decay[i]).astype(BF), v_nb[i], preferred_element_type=F32) for i in idx]
        for i, h in enumerate(hs):
            k_dec_t = k_t[i] * jnp.exp(glast[i] - grow[i])
            state_ref[h] = state[i] * jnp.exp(glast[i]) + jnp.dot(k_dec_t.astype(BF), v_nb[i],
                                                                  preferred_element_type=F32)
        for i, h in enumerate(hs):
            on = o[i] * lax.rsqrt(jnp.mean(o[i] * o[i], -1, keepdims=True) + RMS_EPS) * nw_ref[...]
            z = z_ref[:, h * HEAD_DIM:(h + 1) * HEAD_DIM]
            o_ref[:, h * HEAD_DIM:(h + 1) * HEAD_DIM] = (on * (z * _sigmoid(z))).astype(o_ref.dtype)

    for g in range(0, GDN_HEADS, GDN_GROUP):
        run_heads(list(range(g, g + GDN_GROUP)))

    carry_ref[...] = qkv_ref[n - SUBLANES:n, :]


def _gdn(pm, pt, conv_w, arow, acol, drow, dcol, norm_w, masks, tri, layer):
    s = pm.shape[0]
    n = GDN_TILE
    lsel3 = lambda i: (layer, 0, 0)
    const3 = lambda i: (0, 0, 0)
    return pl.pallas_call(
        _gdn_kernel,
        out_shape=jax.ShapeDtypeStruct((s, GDN_WIDTH), BF),
        grid=(s // n,),
        in_specs=[pl.BlockSpec((n, 3 * GDN_WIDTH), lambda i: (i, COL_GDN_QKV // (3 * GDN_WIDTH))),
                  pl.BlockSpec((n, GDN_WIDTH), lambda i: (i, COL_GDN_Z // GDN_WIDTH)),
                  pl.BlockSpec((n, LANES), lambda i: (i, COL_SMALL // LANES)),
                  pl.BlockSpec((None, GDN_CONV, 3 * GDN_WIDTH), lsel3),
                  pl.BlockSpec((None, 1, LANES), lsel3),
                  pl.BlockSpec((None, LANES, 1), lsel3),
                  pl.BlockSpec((None, 1, LANES), lsel3),
                  pl.BlockSpec((None, LANES, 1), lsel3),
                  pl.BlockSpec((None, 1, HEAD_DIM), lsel3),
                  pl.BlockSpec(masks.shape, const3, pipeline_mode=pl.Buffered(1)),
                  pl.BlockSpec(tri.shape, const3, pipeline_mode=pl.Buffered(1))],
        out_specs=pl.BlockSpec((n, GDN_WIDTH), lambda i: (i, 0)),
        scratch_shapes=[pltpu.VMEM((SUBLANES, 3 * GDN_WIDTH), F32),
                        pltpu.VMEM((GDN_HEADS, HEAD_DIM, HEAD_DIM), F32)],
        compiler_params=_cparams("arbitrary"),
        name="gdn",
    )(pm, pm, pt, conv_w, arow, acol, drow, dcol, norm_w, masks, tri)


def _dsa_prep_kernel(aq_ref, iq_ref, ckv_ref, sm_ref, wk_ref, wvt_ref, kvg_ref, ig_ref, ib_ref,
                     cos_ref, sins_ref, cos_t_ref, sin_t_ref, icos_t_ref, isin_t_ref, icos_ref, isin_ref,
                     qt_ref, k_ref, vt_ref, qit_ref, ki_ref, wt_ref):
    half = HEAD_DIM // 2
    ihalf = IDX_DIM // 2

    aq_t = aq_ref[...].T
    cos_t = cos_t_ref[...]
    sin_t = sin_t_ref[...]
    for h in range(ATT_HEADS):
        x = aq_t[h * HEAD_DIM:(h + 1) * HEAD_DIM, :]
        rot = jnp.concatenate([-x[half:, :], x[:half, :]], axis=0)
        qt_ref[h] = ((x * cos_t + rot * sin_t) * (LOG2E * HEAD_DIM ** -0.5)).astype(qt_ref.dtype)

    iq_t = iq_ref[...].T
    icos_t = icos_t_ref[...]
    isin_t = isin_t_ref[...]
    for h in range(IDX_HEADS):
        x = iq_t[h * IDX_DIM:(h + 1) * IDX_DIM, :]
        rot = jnp.concatenate([-x[ihalf:, :], x[:ihalf, :]], axis=0)
        qit_ref[h * IDX_DIM:(h + 1) * IDX_DIM, :] = (
            (x * icos_t + rot * isin_t) * (IDX_DIM ** -0.5)).astype(qit_ref.dtype)

    c = ckv_ref[...]
    kvn = c * lax.rsqrt(jnp.mean(c * c, -1, keepdims=True) + RMS_EPS) * kvg_ref[...]
    k = jnp.dot(kvn.astype(BF), wk_ref[...], preferred_element_type=F32)
    cos = cos_ref[...]
    sins = sins_ref[...]
    for h in range(ATT_HEADS):
        x = k[:, h * HEAD_DIM:(h + 1) * HEAD_DIM]
        k_ref[h] = (x * cos + pltpu.roll(x, half, axis=1) * sins).astype(k_ref.dtype)
    v_t = jnp.dot(wvt_ref[...], kvn.T.astype(BF), preferred_element_type=F32)
    for h in range(ATT_HEADS):
        vt_ref[h, 0, :HEAD_DIM, :] = v_t[h * HEAD_DIM:(h + 1) * HEAD_DIM, :].astype(vt_ref.dtype)
        vt_ref[h, 0, HEAD_DIM:, :] = jnp.ones((V_ROWS - HEAD_DIM, v_t.shape[1]), vt_ref.dtype)

    sm = sm_ref[...]
    lane = lax.broadcasted_iota(jnp.int32, sm.shape, 1)
    is_k = lane < IDX_DIM
    mu = jnp.sum(jnp.where(is_k, sm, 0.0), -1, keepdims=True) * (1.0 / IDX_DIM)
    d = jnp.where(is_k, sm - mu, 0.0)
    var = jnp.sum(d * d, -1, keepdims=True) * (1.0 / IDX_DIM)
    kin = d * lax.rsqrt(var + LN_EPS) * ig_ref[...] + ib_ref[...]
    below = pltpu.roll(kin, ihalf, axis=1)
    above = pltpu.roll(kin, LANES - ihalf, axis=1)
    rot = jnp.where((lane & (IDX_DIM - 1)) < ihalf, -above, below)
    kir = kin * icos_ref[...] + rot * isin_ref[...]
    ki_ref[...] = kir[:, :IDX_DIM].astype(ki_ref.dtype)
    wt_ref[...] = sm.T[SM_IW:SM_IW + IDX_HEADS, :] * (IDX_HEADS ** -0.5)


def _dsa_prep(p, wk, wvt, kvg, ig, ib, tabs, layer):
    s = p.shape[0]
    tp = KC
    lsel3 = lambda i: (layer, 0, 0)
    row = lambda i: (i, 0)
    col = lambda i: (0, i)
    out_shape = (jax.ShapeDtypeStruct((ATT_HEADS, HEAD_DIM, s), BF),
                 jax.ShapeDtypeStruct((ATT_HEADS, s, HEAD_DIM), BF),
                 jax.ShapeDtypeStruct((ATT_HEADS, s // tp, V_ROWS, tp), BF),
                 jax.ShapeDtypeStruct((IDX_HEADS * IDX_DIM, s), BF),
                 jax.ShapeDtypeStruct((s, IDX_DIM), BF),
                 jax.ShapeDtypeStruct((IDX_HEADS, s), F32))
    return pl.pallas_call(
        _dsa_prep_kernel,
        out_shape=out_shape,
        grid=(s // tp,),
        in_specs=[pl.BlockSpec((tp, ATT_WIDTH), lambda i: (i, COL_ATT_Q // ATT_WIDTH)),
                  pl.BlockSpec((tp, IDX_HEADS * IDX_DIM), lambda i: (i, COL_IDX_Q // (IDX_HEADS * IDX_DIM))),
                  pl.BlockSpec((tp, KV_RANK), lambda i: (i, COL_KV // KV_RANK)),
                  pl.BlockSpec((tp, LANES), lambda i: (i, COL_SMALL // LANES)),
                  pl.BlockSpec((None, KV_RANK, ATT_WIDTH), lsel3),
                  pl.BlockSpec((None, ATT_WIDTH, KV_RANK), lsel3),
                  pl.BlockSpec((None, 1, KV_RANK), lsel3),
                  pl.BlockSpec((None, 1, LANES), lsel3),
                  pl.BlockSpec((None, 1, LANES), lsel3),
                  pl.BlockSpec((tp, HEAD_DIM), row), pl.BlockSpec((tp, HEAD_DIM), row),
                  pl.BlockSpec((HEAD_DIM, tp), col), pl.BlockSpec((HEAD_DIM, tp), col),
                  pl.BlockSpec((IDX_DIM, tp), col), pl.BlockSpec((IDX_DIM, tp), col),
                  pl.BlockSpec((tp, LANES), row), pl.BlockSpec((tp, LANES), row)],
        out_specs=(pl.BlockSpec((ATT_HEADS, HEAD_DIM, tp), lambda i: (0, 0, i)),
                   pl.BlockSpec((ATT_HEADS, tp, HEAD_DIM), lambda i: (0, i, 0)),
                   pl.BlockSpec((ATT_HEADS, 1, V_ROWS, tp), lambda i: (0, i, 0, 0)),
                   pl.BlockSpec((IDX_HEADS * IDX_DIM, tp), col),
                   pl.BlockSpec((tp, IDX_DIM), row),
                   pl.BlockSpec((IDX_HEADS, tp), col)),
        compiler_params=_cparams("parallel"),
        name="dsa_prep",
    )(p, p, p, p, wk, wvt, kvg, ig, ib, *tabs)


def _dsa_kernel(qit_ref, wt_ref, ki_ref, qt_ref, k_ref, vt_ref, o_ref, sc_ref, sh_ref, m_ref, al_ref, acc_ref,
                s_ref, *, top_k):
    i = pl.program_id(0)
    n_chunks = i + 1
    n_pairs = lax.shift_right_logical(n_chunks, 1)

    def chunk(c):
        return pl.ds(pl.multiple_of(c * KC, KC), KC)

    def pair(c2):
        return pl.ds(pl.multiple_of(c2 * (2 * KC), 2 * KC), 2 * KC)

    def high_half(v):
        return lax.bitcast_convert_type(lax.bitcast_convert_type(v, jnp.int32) & jnp.int32(-2 ** 16), F32)

    w_t = wt_ref[...]

    def score_rows(rows, n_rows, first_key):
        kic = ki_ref[rows, :]
        acc = jnp.zeros((n_rows, TQ), F32)
        for h in range(IDX_HEADS):
            d = jnp.dot(kic, qit_ref[h * IDX_DIM:(h + 1) * IDX_DIM, :], preferred_element_type=F32)
            acc = acc + w_t[h:h + 1, :] * jnp.maximum(d, 0.0)
        visible = (first_key + lax.broadcasted_iota(jnp.int32, (n_rows, TQ), 0)
                   <= i * TQ + lax.broadcasted_iota(jnp.int32, (n_rows, TQ), 1))
        score = jnp.where(visible, acc, jnp.nan)
        sc_ref[rows, :] = score
        sh_ref[rows, :] = high_half(score).astype(sh_ref.dtype)

    @pl.loop(0, n_pairs)
    def _score_pairs(c2):
        score_rows(pair(c2), 2 * KC, c2 * (2 * KC))

    @pl.loop(2 * n_pairs, n_chunks)
    def _score_rest(c):
        score_rows(chunk(c), KC, c * KC)

    def count(pred):
        def part(rows):
            one = jnp.where(pred(sc_ref[rows, :]), 1.0, 0.0)
            return jnp.sum(one.reshape(-1, COUNT_ROWS, TQ), axis=0)

        cnt = lax.fori_loop(0, n_pairs, lambda c2, cnt: cnt + part(pair(c2)), jnp.zeros((COUNT_ROWS, TQ), F32))
        cnt = lax.fori_loop(2 * n_pairs, n_chunks, lambda c, cnt: cnt + part(chunk(c)), cnt)
        return jnp.sum(cnt, axis=0, keepdims=True)

    def count_high(thr_h):
        def part(rows):
            one = jnp.where(sh_ref[rows, :] >= thr_h, jnp.ones((), BF), jnp.zeros((), BF))
            one = one.reshape(-1, COUNT_ROWS, TQ)
            tot = one[0]
            for g in range(1, one.shape[0]):
                tot = tot + one[g]
            return tot

        cnt = lax.fori_loop(0, n_pairs, lambda c2, cnt: cnt + part(pair(c2)), jnp.zeros((COUNT_ROWS, TQ), BF))
        cnt = lax.fori_loop(2 * n_pairs, n_chunks, lambda c, cnt: cnt + part(chunk(c)), cnt)
        return jnp.sum(cnt.astype(F32), axis=0, keepdims=True)

    def key_to_f32(cu):
        ks = cu ^ jnp.int32(-2 ** 31)
        bits = jnp.where(ks < 0, ks ^ jnp.int32(2 ** 31 - 1), ks)
        return lax.bitcast_convert_type(bits, F32)

    def bit_pass(it, st, high=False):
        cu, done = st
        cand = cu | lax.shift_left(jnp.int32(1), 31 - it)
        thr_c = key_to_f32(cand)
        cnt = count_high(high_half(thr_c).astype(BF)) if high else count(lambda x: x >= thr_c)
        take = jnp.logical_and(cnt >= top_k, done == 0)
        return jnp.where(take, cand, cu), jnp.where(jnp.logical_and(take, cnt == top_k), 1, done)

    def bits_left(st):
        it, _, done = st
        return jnp.logical_and(it < 32, jnp.min(done) == 0)

    def two_passes(st):
        it, cu, done = st
        cu, done = lax.fori_loop(it, it + 2, bit_pass, (cu, done))
        return it + 2, cu, done

    st = lax.fori_loop(0, 16, functools.partial(bit_pass, high=True),
                       (jnp.zeros((1, TQ), jnp.int32), jnp.zeros((1, TQ), jnp.int32)))
    st = lax.fori_loop(16, UNTESTED_BITS, bit_pass, st)
    _, cu, done = lax.while_loop(bits_left, two_passes, (jnp.int32(UNTESTED_BITS),) + st)
    thr = jnp.where((cu & jnp.int32(-2 ** 23)) == 0, -jnp.inf, key_to_f32(cu))
    has_tie = lax.cond(jnp.min(done) == 1, lambda: jnp.bool_(False),
                       lambda: jnp.max(count(lambda x: x >= thr)) > top_k)

    @pl.when(jnp.logical_not(has_tie))
    def _select():
        def body(c, carry):
            sc_ref[chunk(c), :] = jnp.where(sc_ref[chunk(c), :] >= thr, 0.0, NEG_BIAS)
            return carry
        lax.fori_loop(0, n_chunks, body, 0)

    @pl.when(has_tie)
    def _select_ties():
        need = top_k - count(lambda x: x > thr)
        ltri = jnp.where(lax.broadcasted_iota(jnp.int32, (KC, KC), 1)
                         <= lax.broadcasted_iota(jnp.int32, (KC, KC), 0), 1.0, 0.0).astype(BF)

        def body(c, seen):
            x = sc_ref[chunk(c), :]
            eq = x == thr
            rank = jnp.dot(ltri, jnp.where(eq, 1.0, 0.0).astype(BF), preferred_element_type=F32) + seen
            keep = jnp.logical_or(x > thr, jnp.logical_and(eq, rank <= need))
            sc_ref[chunk(c), :] = jnp.where(keep, 0.0, NEG_BIAS)
            return rank[KC - 1:KC, :]
        lax.fori_loop(0, n_chunks, body, jnp.zeros((1, TQ), F32))

    m_ref[...] = jnp.full(m_ref.shape, -jnp.inf, F32)
    acc_ref[...] = jnp.zeros_like(acc_ref)

    def attend(rows, n_sub, first_chunk):
        n_rows = n_sub * KC
        for h in range(ATT_HEADS):
            s = jnp.dot(k_ref[h, rows, :], qt_ref[h], preferred_element_type=F32) + sc_ref[rows, :]
            s_ref[h, :n_rows, :] = s
            m_old = m_ref[h:h + 1, :]
            m_new = jnp.maximum(m_old, jnp.max(s, axis=0, keepdims=True))
            al_ref[h:h + 1, :] = jnp.exp2(m_old - m_new)
            m_ref[h:h + 1, :] = m_new
        for h in range(ATT_HEADS):
            p = jnp.exp2(s_ref[h, :n_rows, :] - m_ref[h:h + 1, :]).astype(BF)
            pv = jnp.dot(vt_ref[h, first_chunk], p[:KC, :], preferred_element_type=F32)
            for j in range(1, n_sub):
                pv = pv + jnp.dot(vt_ref[h, first_chunk + j], p[j * KC:(j + 1) * KC, :],
                                  preferred_element_type=F32)
            acc_ref[h] = al_ref[h:h + 1, :] * acc_ref[h] + pv

    @pl.loop(0, n_pairs)
    def _attend_pairs(c2):
        attend(pair(c2), 2, 2 * c2)

    @pl.loop(2 * n_pairs, n_chunks)
    def _attend_rest(c):
        attend(chunk(c), 1, c)
    for h in range(ATT_HEADS):
        o_t = acc_ref[h, :HEAD_DIM, :] / acc_ref[h, HEAD_DIM:HEAD_DIM + 1, :]
        o_ref[:, h * HEAD_DIM:(h + 1) * HEAD_DIM] = o_t.T.astype(o_ref.dtype)


def _dsa(qt, k3, vt4, qit, ki, wt):
    s = ki.shape[0]
    top_k = min(INDEX_TOPK, s // 4)
    resident = pl.Buffered(1)
    return pl.pallas_call(
        functools.partial(_dsa_kernel, top_k=top_k),
        out_shape=jax.ShapeDtypeStruct((s, ATT_WIDTH), BF),
        grid=(s // TQ,),
        in_specs=[pl.BlockSpec((IDX_HEADS * IDX_DIM, TQ), lambda i: (0, i)),
                  pl.BlockSpec((IDX_HEADS, TQ), lambda i: (0, i)),
                  pl.BlockSpec((s, IDX_DIM), lambda i: (0, 0), pipeline_mode=resident),
                  pl.BlockSpec((ATT_HEADS, HEAD_DIM, TQ), lambda i: (0, 0, i)),
                  pl.BlockSpec((ATT_HEADS, s, HEAD_DIM), lambda i: (0, 0, 0), pipeline_mode=resident),
                  pl.BlockSpec((ATT_HEADS, s // KC, V_ROWS, KC), lambda i: (0, 0, 0, 0),
                               pipeline_mode=resident)],
        out_specs=pl.BlockSpec((TQ, ATT_WIDTH), lambda i: (i, 0)),
        scratch_shapes=[pltpu.VMEM((s, TQ), F32),
                        pltpu.VMEM((s, TQ), BF),
                        pltpu.VMEM((ATT_HEADS, TQ), F32),
                        pltpu.VMEM((ATT_HEADS, TQ), F32),
                        pltpu.VMEM((ATT_HEADS, V_ROWS, TQ), F32),
                        pltpu.VMEM((ATT_HEADS, 2 * KC, TQ), F32)],
        compiler_params=_cparams("arbitrary"),
        name="dsa",
    )(qit, wt, ki, qt, k3, vt4)


def _layer_norm(r, g, b):
    mu = jnp.mean(r, -1, keepdims=True)
    d = r - mu
    var = jnp.mean(d * d, -1, keepdims=True)
    return d * lax.rsqrt(var + LN_EPS) * g + b


def _out_ln_kernel(a1_ref, a2_ref, w1_ref, w2_ref, x_ref, g_ref, b_ref, xo_ref, xb_ref, wb_ref):
    @pl.when(pl.program_id(0) == 0)
    def _cast():
        wb_ref[0] = w1_ref[...].astype(wb_ref.dtype)
        wb_ref[1] = w2_ref[...].astype(wb_ref.dtype)

    y = (jnp.dot(a1_ref[...], wb_ref[0], preferred_element_type=F32)
         + jnp.dot(a2_ref[...], wb_ref[1], preferred_element_type=F32))
    o = _layer_norm(DN_ALPHA * x_ref[...] + y, g_ref[...], b_ref[...])
    xo_ref[...] = o
    xb_ref[...] = o.astype(xb_ref.dtype)


def _out_ln(o_gdn, o_dsa, w_out, xf, g, b, layer):
    s = xf.shape[0]
    tm = min(512, s)
    row = lambda i: (i, 0)
    lsel3 = lambda i: (layer, 0, 0)
    return pl.pallas_call(
        _out_ln_kernel,
        out_shape=(jax.ShapeDtypeStruct((s, D_MODEL), F32), jax.ShapeDtypeStruct((s, D_MODEL), BF)),
        grid=(s // tm,),
        in_specs=[pl.BlockSpec((tm, GDN_WIDTH), row), pl.BlockSpec((tm, ATT_WIDTH), row),
                  pl.BlockSpec((None, GDN_WIDTH, D_MODEL), lsel3, pipeline_mode=pl.Buffered(1)),
                  pl.BlockSpec((None, ATT_WIDTH, D_MODEL), lambda i: (layer, 1, 0),
                               pipeline_mode=pl.Buffered(1)),
                  pl.BlockSpec((tm, D_MODEL), row),
                  pl.BlockSpec((None, 1, D_MODEL), lsel3), pl.BlockSpec((None, 1, D_MODEL), lsel3)],
        out_specs=(pl.BlockSpec((tm, D_MODEL), row), pl.BlockSpec((tm, D_MODEL), row)),
        scratch_shapes=[pltpu.VMEM((2, GDN_WIDTH, D_MODEL), BF)],
        compiler_params=_cparams("arbitrary"),
        name="out_ln",
    )(o_gdn, o_dsa, w_out, w_out, xf, g, b)


def _ffn_up_kernel(x_ref, wg_ref, wv_ref, cg_ref, cv_ref, bg_ref, bv_ref, h_ref, carry_ref, wb_ref):
    tm, tn = h_ref.shape

    @pl.when(pl.program_id(1) == 0)
    def _init():
        carry_ref[...] = jnp.zeros_like(carry_ref)
        wb_ref[0] = wg_ref[...].astype(wb_ref.dtype)
        wb_ref[1] = wv_ref[...].astype(wb_ref.dtype)

    row8 = lax.broadcasted_iota(jnp.int32, (SUBLANES, h_ref.shape[1]), 0)

    def conv(u, prev, w, b):
        acc = u * w[FFN_CONV - 1:FFN_CONV, :] + b
        for s in range(1, FFN_CONV):
            us = pltpu.roll(u, s, axis=0)
            top = jnp.where(row8 < s, pltpu.roll(prev, s, axis=0), us[0:SUBLANES, :])
            us = jnp.concatenate([top, us[SUBLANES:, :]], axis=0)
            acc = acc + us * w[FFN_CONV - 1 - s:FFN_CONV - s, :]
        return acc

    x = x_ref[...]
    ug = jnp.dot(x, wb_ref[0], preferred_element_type=F32)
    uv = jnp.dot(x, wb_ref[1], preferred_element_type=F32)
    gate = conv(ug, carry_ref[0], cg_ref[...], bg_ref[...])
    val = conv(uv, carry_ref[1], cv_ref[...], bv_ref[...])
    carry_ref[0] = ug[tm - SUBLANES:, :]
    carry_ref[1] = uv[tm - SUBLANES:, :]
    h_ref[...] = (gate * _sigmoid(gate) * val).astype(h_ref.dtype)


def _ffn_up(xb, w_up, conv_w, conv_b, layer):
    s = xb.shape[0]
    tm = min(1024, s)
    tn = 512
    nj = D_FF // tn
    return pl.pallas_call(
        _ffn_up_kernel,
        out_shape=jax.ShapeDtypeStruct((s, D_FF), BF),
        grid=(nj, s // tm),
        in_specs=[pl.BlockSpec((tm, D_MODEL), lambda j, i: (i, 0)),
                  pl.BlockSpec((None, D_MODEL, tn), lambda j, i: (layer, 0, j)),
                  pl.BlockSpec((None, D_MODEL, tn), lambda j, i: (layer, 0, j + nj)),
                  pl.BlockSpec((None, FFN_CONV, tn), lambda j, i: (layer, 0, j)),
                  pl.BlockSpec((None, FFN_CONV, tn), lambda j, i: (layer, 0, j + nj)),
                  pl.BlockSpec((None, 1, tn), lambda j, i: (layer, 0, j)),
                  pl.BlockSpec((None, 1, tn), lambda j, i: (layer, 0, j + nj))],
        out_specs=pl.BlockSpec((tm, tn), lambda j, i: (i, j)),
        scratch_shapes=[pltpu.VMEM((2, SUBLANES, tn), F32),
                        pltpu.VMEM((2, D_MODEL, tn), BF)],
        compiler_params=_cparams("arbitrary", "arbitrary"),
        name="ffn_up",
    )(xb, w_up, w_up, conv_w, conv_w, conv_b, conv_b)


def _ffn_down_kernel(h_ref, w_ref, x_ref, g_ref, b_ref, xo_ref, xb_ref):
    f = jnp.dot(h_ref[...], w_ref[...], preferred_element_type=F32)
    o = _layer_norm(DN_ALPHA * x_ref[...] + f, g_ref[...], b_ref[...])
    xo_ref[...] = o
    xb_ref[...] = o.astype(xb_ref.dtype)


def _ffn_down_ln(h, w_down, xf, g, b, layer):
    s = xf.shape[0]
    tm = min(256, s)
    row = lambda i: (i, 0)
    lsel3 = lambda i: (layer, 0, 0)
    return pl.pallas_call(
        _ffn_down_kernel,
        out_shape=(jax.ShapeDtypeStruct((s, D_MODEL), F32), jax.ShapeDtypeStruct((s, D_MODEL), BF)),
        grid=(s // tm,),
        in_specs=[pl.BlockSpec((tm, D_FF), row),
                  pl.BlockSpec((None, D_FF, D_MODEL), lsel3, pipeline_mode=pl.Buffered(1)),
                  pl.BlockSpec((tm, D_MODEL), row),
                  pl.BlockSpec((None, 1, D_MODEL), lsel3), pl.BlockSpec((None, 1, D_MODEL), lsel3)],
        out_specs=(pl.BlockSpec((tm, D_MODEL), row), pl.BlockSpec((tm, D_MODEL), row)),
        compiler_params=_cparams("parallel"),
        name="ffn_down",
    )(h, w_down, xf, g, b)


def _rope_tables(seq, dim):
    inv = ROPE_THETA ** (-jnp.arange(0, dim, 2, dtype=F32) / dim)
    ang = jnp.arange(seq, dtype=F32)[:, None] * inv[None, :]
    ang = jnp.concatenate([ang, ang], -1)
    return jnp.cos(ang), jnp.sin(ang)


def _lane_pad(v, offset):
    out = jnp.zeros((v.shape[0], 1, LANES), F32)
    return out.at[:, 0, offset:offset + v.shape[1]].set(v.astype(F32))


def kernel(x, w_in, gdn_conv_w, gdn_a_log, gdn_dt_bias, gdn_norm_w, kv_norm_w, w_ukv, idx_k_norm_g,
           idx_k_norm_b, w_out, ln1_g, ln1_b, ffn_up, ffn_conv_w, ffn_conv_b, ffn_down, ln2_g, ln2_b):
    seq = x.shape[1]
    depth = w_in.shape[0]

    o_ga = MAIN_WIDTH
    o_aq = o_ga + 2 * GDN_HEADS
    o_ckv = o_aq + ATT_WIDTH
    o_iq = o_ckv + KV_RANK
    o_ik = o_iq + IDX_HEADS * IDX_DIM
    o_iw = o_ik + IDX_DIM
    o_end = o_iw + IDX_HEADS
    w_main = w_in[:, :, :o_ga].astype(BF)
    w_rest = w_in[:, :, o_ga:].astype(BF)
    cut = lambda lo, hi: w_rest[:, :, lo - o_ga:hi - o_ga]
    w_tail = jnp.concatenate(
        [cut(o_aq, o_ckv), cut(o_iq, o_ik), cut(o_ckv, o_iq), cut(o_ik, o_end), cut(o_ga, o_aq),
         jnp.zeros((depth, D_MODEL, TAIL_WIDTH - (o_end - o_ga)), BF)], axis=-1)
    wk = w_ukv[:, :, :ATT_WIDTH].astype(BF)
    wvt = jnp.swapaxes(w_ukv[:, :, ATT_WIDTH:], 1, 2).astype(BF)
    w_down_b = ffn_down.astype(BF)

    arow = _lane_pad(gdn_a_log, SM_GA)
    drow = _lane_pad(gdn_dt_bias, SM_GA)
    acol = jnp.swapaxes(arow, 1, 2)
    dcol = jnp.swapaxes(drow, 1, 2)
    ig = _lane_pad(idx_k_norm_g, 0)
    ib = _lane_pad(idx_k_norm_b, 0)
    r3 = lambda v: v.astype(F32)[:, None, :]

    cos, sin = _rope_tables(seq, HEAD_DIM)
    icos, isin = _rope_tables(seq, IDX_DIM)
    sign = jnp.where(jnp.arange(HEAD_DIM) < HEAD_DIM // 2, -1.0, 1.0).astype(F32)
    lane_fill = lambda t: jnp.concatenate([t, jnp.zeros_like(t)], axis=-1)
    tabs = (cos, sin * sign, cos.T, sin.T, icos.T, isin.T, lane_fill(icos), lane_fill(isin))

    masks, tri = _gdn_masks()

    xf = x[0]
    xb = xf.astype(BF)
    for layer in range(depth):
        pm = _proj_main(xb, w_main, layer)
        pt = _proj_tail(xb, w_tail, layer)
        o_gdn = _gdn(pm, pt, gdn_conv_w, arow, acol, drow, dcol, r3(gdn_norm_w), masks, tri, layer)
        qt, k3, vt4, qit, ki, wt = _dsa_prep(pt, wk, wvt, r3(kv_norm_w), ig, ib, tabs, layer)
        o_dsa = _dsa(qt, k3, vt4, qit, ki, wt)
        xf, xb = _out_ln(o_gdn, o_dsa, w_out, xf, r3(ln1_g), r3(ln1_b), layer)
        h = _ffn_up(xb, ffn_up, ffn_conv_w, r3(ffn_conv_b), layer)
        xf, xb = _ffn_down_ln(h, w_down_b, xf, r3(ln2_g), r3(ln2_b), layer)
    return xf[None]
```

```python
import functools

import jax
import jax.numpy as jnp
import numpy as np
from jax import lax
from jax.experimental import pallas as pl
from jax.experimental.pallas import tpu as pltpu

D_MODEL = 2048
DEPTH = 4
HEAD_DIM = 128
GDN_HEADS = 8
ATT_HEADS = 8
GDN_WIDTH = GDN_HEADS * HEAD_DIM
ATT_WIDTH = ATT_HEADS * HEAD_DIM
GDN_CONV = 4
KV_RANK = 256
IDX_HEADS = 16
IDX_DIM = 64
INDEX_TOPK = 256
ROPE_THETA = 10000.0
D_FF = 5632
FFN_CONV = 3
LN_EPS = 1e-5
RMS_EPS = 1e-6
DN_ALPHA = (2 * DEPTH) ** 0.25

BF = jnp.bfloat16
F32 = jnp.float32
HIGHEST = lax.Precision.HIGHEST

LANES = 128
SUBLANES = 8
VMEM_LIMIT_BYTES = 58 * 1024 * 1024

MAIN_WIDTH = 4 * GDN_WIDTH
COL_GDN_QKV = 0
COL_GDN_Z = 3 * GDN_WIDTH
COL_ATT_Q = 0
COL_IDX_Q = COL_ATT_Q + ATT_WIDTH
COL_KV = COL_IDX_Q + IDX_HEADS * IDX_DIM
COL_SMALL = COL_KV + KV_RANK
SM_IW = IDX_DIM
SM_GA = SM_IW + IDX_HEADS
SM_GB = SM_GA + GDN_HEADS
TAIL_WIDTH = COL_SMALL + 2 * LANES

GDN_TILE = 256
INV_BASE = 16
GDN_GROUP = 8
TQ = 256
KC = 256
COUNT_ROWS = 64
UNTESTED_BITS = 24
NEG_BIAS = -1e30
V_ROWS = HEAD_DIM + 16
LOG2E = 1.4426950408889634


def _cparams(*sem):
    return pltpu.CompilerParams(dimension_semantics=sem, vmem_limit_bytes=VMEM_LIMIT_BYTES)


def _sigmoid(x):
    return 1.0 / (1.0 + jnp.exp(-x))


def _softplus(x):
    return jnp.maximum(x, 0.0) + jnp.log(1.0 + jnp.exp(-jnp.abs(x)))


def _dot(a, b):
    return jnp.dot(a.astype(BF), b.astype(BF), preferred_element_type=F32)


def _mm_kernel(a_ref, b_ref, o_ref):
    o_ref[...] = jnp.dot(a_ref[...], b_ref[...], preferred_element_type=F32)


def _proj_main(xb, w_main, layer):
    s, k = xb.shape
    tm = min(1024, s)
    tn = 1024
    return pl.pallas_call(
        _mm_kernel,
        out_shape=jax.ShapeDtypeStruct((s, MAIN_WIDTH), F32),
        grid=(MAIN_WIDTH // tn, s // tm),
        in_specs=[pl.BlockSpec((tm, k), lambda j, i: (i, 0)),
                  pl.BlockSpec((None, k, tn), lambda j, i: (layer, 0, j))],
        out_specs=pl.BlockSpec((tm, tn), lambda j, i: (i, j)),
        compiler_params=_cparams("parallel", "parallel"),
        name="proj_main",
    )(xb, w_main)


def _proj_tail(xb, w_tail, layer):
    s, k = xb.shape
    tm = min(1024, s)
    tn = TAIL_WIDTH // 2
    return pl.pallas_call(
        _mm_kernel,
        out_shape=jax.ShapeDtypeStruct((s, TAIL_WIDTH), F32),
        grid=(TAIL_WIDTH // tn, s // tm),
        in_specs=[pl.BlockSpec((tm, k), lambda j, i: (i, 0)),
                  pl.BlockSpec((None, k, tn), lambda j, i: (layer, 0, j))],
        out_specs=pl.BlockSpec((tm, tn), lambda j, i: (i, j)),
        compiler_params=_cparams("parallel", "parallel"),
        name="proj_tail",
    )(xb, w_tail)


M_LOWER, M_STRICT, M_EYE, M_BASE, M_MERGE0 = 0, 1, 2, 3, 4
N_MERGE = int(np.log2(GDN_TILE // INV_BASE))


def _gdn_masks():
    r = np.arange(GDN_TILE)[:, None]
    c = np.arange(GDN_TILE)[None, :]
    same = lambda size: (r // size) == (c // size)
    rows = [c <= r, c < r, c == r, same(INV_BASE)]
    rows += [same(2 * INV_BASE << i) & ~same(INV_BASE << i) for i in range(N_MERGE)]
    masks = jnp.asarray(np.stack(rows).astype(np.float32))
    tri = jnp.asarray(np.stack([c <= r, r <= c]).astype(np.float32)).astype(BF)
    return masks, tri


def _split3(x):
    h1 = x.astype(BF)
    r1 = x - h1.astype(F32)
    h2 = r1.astype(BF)
    return h1, h2, (r1 - h2.astype(F32)).astype(BF)


def _unit_lower_inverses(a, mask_ref):
    nk = [-(x * mask_ref[M_BASE]) for x in a]
    t = [mask_ref[M_EYE] + x for x in nk]
    for _ in range(int(np.log2(INV_BASE)) - 1):
        nk = [_dot(x, x) for x in nk]
        t = [x + _dot(x, y) for x, y in zip(t, nk)]
    for lvl in range(N_MERGE):
        et = [_dot(x * mask_ref[M_MERGE0 + lvl], y) for x, y in zip(a, t)]
        t = [x - _dot(x, y) for x, y in zip(t, et)]
    return t


def _gdn_kernel(qkv_ref, z_ref, sm_ref, cw_ref, arow_ref, acol_ref, drow_ref, dcol_ref, nw_ref, mask_ref, tri_ref,
                o_ref, carry_ref, state_ref):
    n = GDN_TILE

    @pl.when(pl.program_id(0) == 0)
    def _init():
        carry_ref[...] = jnp.zeros_like(carry_ref)
        state_ref[...] = jnp.zeros_like(state_ref)

    row8 = lax.broadcasted_iota(jnp.int32, (SUBLANES, LANES), 0)

    sm = sm_ref[...]
    g_cols = -jnp.exp(arow_ref[...]) * _softplus(sm + drow_ref[...])
    beta_cols = _sigmoid(sm)
    sm_t = sm.T
    ga = slice(SM_GA, SM_GA + GDN_HEADS)
    g_rows = -jnp.exp(acol_ref[ga, :]) * _softplus(sm_t[ga, :] + dcol_ref[ga, :])
    gc_cols = sum(jnp.dot(tri_ref[0], part, preferred_element_type=F32) for part in _split3(g_cols))
    gc_rows = sum(jnp.dot(part, tri_ref[1], preferred_element_type=F32) for part in _split3(g_rows))

    def conv_silu(off):
        x = qkv_ref[:, off:off + LANES]
        prev = carry_ref[:, off:off + LANES]
        w = cw_ref[:, off:off + LANES]
        acc = x * w[GDN_CONV - 1:GDN_CONV, :]
        for s in range(1, GDN_CONV):
            xs = pltpu.roll(x, s, axis=0)
            top = jnp.where(row8 < s, pltpu.roll(prev, s, axis=0), xs[0:SUBLANES, :])
            xs = jnp.concatenate([top, xs[SUBLANES:, :]], axis=0)
            acc = acc + xs * w[GDN_CONV - 1 - s:GDN_CONV - s, :]
        return acc * _sigmoid(acc)

    def run_heads(hs):
        idx = range(len(hs))
        q = [conv_silu(h * HEAD_DIM) for h in hs]
        k = [conv_silu(GDN_WIDTH + h * HEAD_DIM) for h in hs]
        v = [conv_silu(2 * GDN_WIDTH + h * HEAD_DIM) for h in hs]
        q = [x * lax.rsqrt(jnp.sum(x * x, -1, keepdims=True) + RMS_EPS) * (HEAD_DIM ** -0.5) for x in q]
        k = [x * lax.rsqrt(jnp.sum(x * x, -1, keepdims=True) + RMS_EPS) for x in k]
        gcol = [gc_cols[:, SM_GA + h:SM_GA + h + 1] for h in hs]
        grow = [gc_rows[h:h + 1, :] for h in hs]
        bcol = [beta_cols[:, SM_GB + h:SM_GB + h + 1] for h in hs]
        glast = [x[n - 1:n, :] for x in gcol]
        ecol = [jnp.exp(x) for x in gcol]
        decay = [mask_ref[M_LOWER] * jnp.exp(jnp.minimum(c - r, 0.0)) for c, r in zip(gcol, grow)]
        k_t = [x.T for x in k]
        k_tb = [x.astype(BF) for x in k_t]
        kk = [jnp.dot(x.astype(BF), y, preferred_element_type=F32) for x, y in zip(k, k_tb)]
        qk = [jnp.dot(x.astype(BF), y, preferred_element_type=F32) for x, y in zip(q, k_tb)]
        a = [mask_ref[M_STRICT] * (b * x * d) for b, x, d in zip(bcol, kk, decay)]
        t = _unit_lower_inverses(a, mask_ref)
        rhs = [jnp.concatenate([v[i] * bcol[i], k[i] * (bcol[i] * ecol[i])], axis=1) for i in idx]
        sol = [_dot(x, y) for x, y in zip(t, rhs)]
        state = [state_ref[h] for h in hs]
        state_b = [x.astype(BF) for x in state]
        v_new = [x[:, :HEAD_DIM] - jnp.dot(x[:, HEAD_DIM:].astype(BF), s, preferred_element_type=F32)
                 for x, s in zip(sol, state_b)]
        v_nb = [x.astype(BF) for x in v_new]
        o = [jnp.dot((q[i] * ecol[i]).astype(BF), state_b[i], preferred_element_type=F32)
             + jnp.dot((qk[i] * decay[i]).astype(BF), v_nb[i], preferred_element_type=F32) for i in idx]
        for i, h in enumerate(hs):
            k_dec_t = k_t[i] * jnp.exp(glast[i] - grow[i])
            state_ref[h] = state[i] * jnp.exp(glast[i]) + jnp.dot(k_dec_t.astype(BF), v_nb[i],
                                                                  preferred_element_type=F32)
        for i, h in enumerate(hs):
            on = o[i] * lax.rsqrt(jnp.mean(o[i] * o[i], -1, keepdims=True) + RMS_EPS) * nw_ref[...]
            z = z_ref[:, h * HEAD_DIM:(h + 1) * HEAD_DIM]
            o_ref[:, h * HEAD_DIM:(h + 1) * HEAD_DIM] = (on * (z * _sigmoid(z))).astype(o_ref.dtype)

    for g in range(0, GDN_HEADS, GDN_GROUP):
        run_heads(list(range(g, g + GDN_GROUP)))

    carry_ref[...] = qkv_ref[n - SUBLANES:n, :]


def _gdn(pm, pt, conv_w, arow, acol, drow, dcol, norm_w, masks, tri, layer):
    s = pm.shape[0]
    n = GDN_TILE
    lsel3 = lambda i: (layer, 0, 0)
    const3 = lambda i: (0, 0, 0)
    return pl.pallas_call(
        _gdn_kernel,
        out_shape=jax.ShapeDtypeStruct((s, GDN_WIDTH), BF),
        grid=(s // n,),
        in_specs=[pl.BlockSpec((n, 3 * GDN_WIDTH), lambda i: (i, COL_GDN_QKV // (3 * GDN_WIDTH))),
                  pl.BlockSpec((n, GDN_WIDTH), lambda i: (i, COL_GDN_Z // GDN_WIDTH)),
                  pl.BlockSpec((n, LANES), lambda i: (i, COL_SMALL // LANES)),
                  pl.BlockSpec((None, GDN_CONV, 3 * GDN_WIDTH), lsel3),
                  pl.BlockSpec((None, 1, LANES), lsel3),
                  pl.BlockSpec((None, LANES, 1), lsel3),
                  pl.BlockSpec((None, 1, LANES), lsel3),
                  pl.BlockSpec((None, LANES, 1), lsel3),
                  pl.BlockSpec((None, 1, HEAD_DIM), lsel3),
                  pl.BlockSpec(masks.shape, const3, pipeline_mode=pl.Buffered(1)),
                  pl.BlockSpec(tri.shape, const3, pipeline_mode=pl.Buffered(1))],
        out_specs=pl.BlockSpec((n, GDN_WIDTH), lambda i: (i, 0)),
        scratch_shapes=[pltpu.VMEM((SUBLANES, 3 * GDN_WIDTH), F32),
                        pltpu.VMEM((GDN_HEADS, HEAD_DIM, HEAD_DIM), F32)],
        compiler_params=_cparams("arbitrary"),
        name="gdn",
    )(pm, pm, pt, conv_w, arow, acol, drow, dcol, norm_w, masks, tri)


def _dsa_prep_kernel(aq_ref, iq_ref, ckv_ref, sm_ref, wk_ref, wvt_ref, kvg_ref, ig_ref, ib_ref,
                     cos_ref, sins_ref, cos_t_ref, sin_t_ref, icos_t_ref, isin_t_ref, icos_ref, isin_ref,
                     qt_ref, k_ref, vt_ref, qit_ref, ki_ref, wt_ref):
    half = HEAD_DIM // 2
    ihalf = IDX_DIM // 2

    aq_t = aq_ref[...].T
    cos_t = cos_t_ref[...]
    sin_t = sin_t_ref[...]
    for h in range(ATT_HEADS):
        x = aq_t[h * HEAD_DIM:(h + 1) * HEAD_DIM, :]
        rot = jnp.concatenate([-x[half:, :], x[:half, :]], axis=0)
        qt_ref[h] = ((x * cos_t + rot * sin_t) * (LOG2E * HEAD_DIM ** -0.5)).astype(qt_ref.dtype)

    iq_t = iq_ref[...].T
    icos_t = icos_t_ref[...]
    isin_t = isin_t_ref[...]
    for h in range(IDX_HEADS):
        x = iq_t[h * IDX_DIM:(h + 1) * IDX_DIM, :]
        rot = jnp.concatenate([-x[ihalf:, :], x[:ihalf, :]], axis=0)
        qit_ref[h * IDX_DIM:(h + 1) * IDX_DIM, :] = (
            (x * icos_t + rot * isin_t) * (IDX_DIM ** -0.5)).astype(qit_ref.dtype)

    c = ckv_ref[...]
    kvn = c * lax.rsqrt(jnp.mean(c * c, -1, keepdims=True) + RMS_EPS) * kvg_ref[...]
    k = jnp.dot(kvn.astype(BF), wk_ref[...], preferred_element_type=F32)
    cos = cos_ref[...]
    sins = sins_ref[...]
    for h in range(ATT_HEADS):
        x = k[:, h * HEAD_DIM:(h + 1) * HEAD_DIM]
        k_ref[h] = (x * cos + pltpu.roll(x, half, axis=1) * sins).astype(k_ref.dtype)
    v_t = jnp.dot(wvt_ref[...], kvn.T.astype(BF), preferred_element_type=F32)
    for h in range(ATT_HEADS):
        vt_ref[h, 0, :HEAD_DIM, :] = v_t[h * HEAD_DIM:(h + 1) * HEAD_DIM, :].astype(vt_ref.dtype)
        vt_ref[h, 0, HEAD_DIM:, :] = jnp.ones((V_ROWS - HEAD_DIM, v_t.shape[1]), vt_ref.dtype)

    sm = sm_ref[...]
    lane = lax.broadcasted_iota(jnp.int32, sm.shape, 1)
    is_k = lane < IDX_DIM
    mu = jnp.sum(jnp.where(is_k, sm, 0.0), -1, keepdims=True) * (1.0 / IDX_DIM)
    d = jnp.where(is_k, sm - mu, 0.0)
    var = jnp.sum(d * d, -1, keepdims=True) * (1.0 / IDX_DIM)
    kin = d * lax.rsqrt(var + LN_EPS) * ig_ref[...] + ib_ref[...]
    below = pltpu.roll(kin, ihalf, axis=1)
    above = pltpu.roll(kin, LANES - ihalf, axis=1)
    rot = jnp.where((lane & (IDX_DIM - 1)) < ihalf, -above, below)
    kir = kin * icos_ref[...] + rot * isin_ref[...]
    ki_ref[...] = kir[:, :IDX_DIM].astype(ki_ref.dtype)
    wt_ref[...] = sm.T[SM_IW:SM_IW + IDX_HEADS, :] * (IDX_HEADS ** -0.5)


def _dsa_prep(p, wk, wvt, kvg, ig, ib, tabs, layer):
    s = p.shape[0]
    tp = KC
    lsel3 = lambda i: (layer, 0, 0)
    row = lambda i: (i, 0)
    col = lambda i: (0, i)
    out_shape = (jax.ShapeDtypeStruct((ATT_HEADS, HEAD_DIM, s), BF),
                 jax.ShapeDtypeStruct((ATT_HEADS, s, HEAD_DIM), BF),
                 jax.ShapeDtypeStruct((ATT_HEADS, s // tp, V_ROWS, tp), BF),
                 jax.ShapeDtypeStruct((IDX_HEADS * IDX_DIM, s), BF),
                 jax.ShapeDtypeStruct((s, IDX_DIM), BF),
                 jax.ShapeDtypeStruct((IDX_HEADS, s), F32))
    return pl.pallas_call(
        _dsa_prep_kernel,
        out_shape=out_shape,
        grid=(s // tp,),
        in_specs=[pl.BlockSpec((tp, ATT_WIDTH), lambda i: (i, COL_ATT_Q // ATT_WIDTH)),
                  pl.BlockSpec((tp, IDX_HEADS * IDX_DIM), lambda i: (i, COL_IDX_Q // (IDX_HEADS * IDX_DIM))),
                  pl.BlockSpec((tp, KV_RANK), lambda i: (i, COL_KV // KV_RANK)),
                  pl.BlockSpec((tp, LANES), lambda i: (i, COL_SMALL // LANES)),
                  pl.BlockSpec((None, KV_RANK, ATT_WIDTH), lsel3),
                  pl.BlockSpec((None, ATT_WIDTH, KV_RANK), lsel3),
                  pl.BlockSpec((None, 1, KV_RANK), lsel3),
                  pl.BlockSpec((None, 1, LANES), lsel3),
                  pl.BlockSpec((None, 1, LANES), lsel3),
                  pl.BlockSpec((tp, HEAD_DIM), row), pl.BlockSpec((tp, HEAD_DIM), row),
                  pl.BlockSpec((HEAD_DIM, tp), col), pl.BlockSpec((HEAD_DIM, tp), col),
                  pl.BlockSpec((IDX_DIM, tp), col), pl.BlockSpec((IDX_DIM, tp), col),
                  pl.BlockSpec((tp, LANES), row), pl.BlockSpec((tp, LANES), row)],
        out_specs=(pl.BlockSpec((ATT_HEADS, HEAD_DIM, tp), lambda i: (0, 0, i)),
                   pl.BlockSpec((ATT_HEADS, tp, HEAD_DIM), lambda i: (0, i, 0)),
                   pl.BlockSpec((ATT_HEADS, 1, V_ROWS, tp), lambda i: (0, i, 0, 0)),
                   pl.BlockSpec((IDX_HEADS * IDX_DIM, tp), col),
                   pl.BlockSpec((tp, IDX_DIM), row),
                   pl.BlockSpec((IDX_HEADS, tp), col)),
        compiler_params=_cparams("parallel"),
        name="dsa_prep",
    )(p, p, p, p, wk, wvt, kvg, ig, ib, *tabs)


def _dsa_kernel(qit_ref, wt_ref, ki_ref, qt_ref, k_ref, vt_ref, o_ref, sc_ref, sh_ref, m_ref, al_ref, acc_ref,
                s_ref, *, top_k):
    i = pl.program_id(0)
    n_chunks = i + 1
    n_pairs = lax.shift_right_logical(n_chunks, 1)

    def chunk(c):
        return pl.ds(pl.multiple_of(c * KC, KC), KC)

    def pair(c2):
        return pl.ds(pl.multiple_of(c2 * (2 * KC), 2 * KC), 2 * KC)

    def high_half(v):
        return lax.bitcast_convert_type(lax.bitcast_convert_type(v, jnp.int32) & jnp.int32(-2 ** 16), F32)

    w_t = wt_ref[...]

    def score_rows(rows, n_rows, first_key):
        kic = ki_ref[rows, :]
        acc = jnp.zeros((n_rows, TQ), F32)
        for h in range(IDX_HEADS):
            d = jnp.dot(kic, qit_ref[h * IDX_DIM:(h + 1) * IDX_DIM, :], preferred_element_type=F32)
            acc = acc + w_t[h:h + 1, :] * jnp.maximum(d, 0.0)
        visible = (first_key + lax.broadcasted_iota(jnp.int32, (n_rows, TQ), 0)
                   <= i * TQ + lax.broadcasted_iota(jnp.int32, (n_rows, TQ), 1))
        score = jnp.where(visible, acc, jnp.nan)
        sc_ref[rows, :] = score
        sh_ref[rows, :] = high_half(score).astype(sh_ref.dtype)

    n_quads = lax.shift_right_logical(n_chunks, 2)

    @pl.loop(0, n_quads)
    def _score_quads(c4):
        score_rows(pl.ds(pl.multiple_of(c4 * (4 * KC), 4 * KC), 4 * KC), 4 * KC, c4 * (4 * KC))

    @pl.loop(2 * n_quads, n_pairs)
    def _score_pairs(c2):
        score_rows(pair(c2), 2 * KC, c2 * (2 * KC))

    @pl.loop(2 * n_pairs, n_chunks)
    def _score_rest(c):
        score_rows(chunk(c), KC, c * KC)

    def count(pred):
        def part(rows):
            one = jnp.where(pred(sc_ref[rows, :]), 1.0, 0.0)
            return jnp.sum(one.reshape(-1, COUNT_ROWS, TQ), axis=0)

        cnt = lax.fori_loop(0, n_pairs, lambda c2, cnt: cnt + part(pair(c2)), jnp.zeros((COUNT_ROWS, TQ), F32))
        cnt = lax.fori_loop(2 * n_pairs, n_chunks, lambda c, cnt: cnt + part(chunk(c)), cnt)
        return jnp.sum(cnt, axis=0, keepdims=True)

    def count_high(thr_h):
        def part(rows):
            one = jnp.where(sh_ref[rows, :] >= thr_h, jnp.ones((), BF), jnp.zeros((), BF))
            one = one.reshape(-1, COUNT_ROWS, TQ)
            tot = one[0]
            for g in range(1, one.shape[0]):
                tot = tot + one[g]
            return tot

        cnt = lax.fori_loop(0, n_pairs, lambda c2, cnt: cnt + part(pair(c2)), jnp.zeros((COUNT_ROWS, TQ), BF))
        cnt = lax.fori_loop(2 * n_pairs, n_chunks, lambda c, cnt: cnt + part(chunk(c)), cnt)
        return jnp.sum(cnt.astype(F32), axis=0, keepdims=True)

    def key_to_f32(cu):
        ks = cu ^ jnp.int32(-2 ** 31)
        bits = jnp.where(ks < 0, ks ^ jnp.int32(2 ** 31 - 1), ks)
        return lax.bitcast_convert_type(bits, F32)

    def bit_pass(it, st, high=False):
        cu, done = st
        cand = cu | lax.shift_left(jnp.int32(1), 31 - it)
        thr_c = key_to_f32(cand)
        cnt = count_high(high_half(thr_c).astype(BF)) if high else count(lambda x: x >= thr_c)
        take = jnp.logical_and(cnt >= top_k, done == 0)
        return jnp.where(take, cand, cu), jnp.where(jnp.logical_and(take, cnt == top_k), 1, done)

    def bits_left(st):
        it, _, done = st
        return jnp.logical_and(it < 32, jnp.min(done) == 0)

    def two_passes(st):
        it, cu, done = st
        cu, done = lax.fori_loop(it, it + 2, bit_pass, (cu, done))
        return it + 2, cu, done

    st = lax.fori_loop(0, 16, functools.partial(bit_pass, high=True),
                       (jnp.zeros((1, TQ), jnp.int32), jnp.zeros((1, TQ), jnp.int32)))
    st = lax.fori_loop(16, UNTESTED_BITS, bit_pass, st)
    _, cu, done = lax.while_loop(bits_left, two_passes, (jnp.int32(UNTESTED_BITS),) + st)
    thr = jnp.where((cu & jnp.int32(-2 ** 23)) == 0, -jnp.inf, key_to_f32(cu))
    has_tie = lax.cond(jnp.min(done) == 1, lambda: jnp.bool_(False),
                       lambda: jnp.max(count(lambda x: x >= thr)) > top_k)

    @pl.when(jnp.logical_not(has_tie))
    def _select():
        def body(c, carry):
            sc_ref[chunk(c), :] = jnp.where(sc_ref[chunk(c), :] >= thr, 0.0, NEG_BIAS)
            return carry
        lax.fori_loop(0, n_chunks, body, 0)

    @pl.when(has_tie)
    def _select_ties():
        need = top_k - count(lambda x: x > thr)
        ltri = jnp.where(lax.broadcasted_iota(jnp.int32, (KC, KC), 1)
                         <= lax.broadcasted_iota(jnp.int32, (KC, KC), 0), 1.0, 0.0).astype(BF)

        def body(c, seen):
            x = sc_ref[chunk(c), :]
            eq = x == thr
            rank = jnp.dot(ltri, jnp.where(eq, 1.0, 0.0).astype(BF), preferred_element_type=F32) + seen
            keep = jnp.logical_or(x > thr, jnp.logical_and(eq, rank <= need))
            sc_ref[chunk(c), :] = jnp.where(keep, 0.0, NEG_BIAS)
            return rank[KC - 1:KC, :]
        lax.fori_loop(0, n_chunks, body, jnp.zeros((1, TQ), F32))

    m_ref[...] = jnp.full(m_ref.shape, -jnp.inf, F32)
    acc_ref[...] = jnp.zeros_like(acc_ref)

    def attend(rows, n_sub, first_chunk):
        n_rows = n_sub * KC
        for h in range(ATT_HEADS):
            s = jnp.dot(k_ref[h, rows, :], qt_ref[h], preferred_element_type=F32) + sc_ref[rows, :]
            s_ref[h, :n_rows, :] = s
            m_old = m_ref[h:h + 1, :]
            m_new = jnp.maximum(m_old, jnp.max(s, axis=0, keepdims=True))
            al_ref[h:h + 1, :] = jnp.exp2(m_old - m_new)
            m_ref[h:h + 1, :] = m_new
        for h in range(ATT_HEADS):
            p = jnp.exp2(s_ref[h, :n_rows, :] - m_ref[h:h + 1, :]).astype(BF)
            pv = jnp.dot(vt_ref[h, first_chunk], p[:KC, :], preferred_element_type=F32)
            for j in range(1, n_sub):
                pv = pv + jnp.dot(vt_ref[h, first_chunk + j], p[j * KC:(j + 1) * KC, :],
                                  preferred_element_type=F32)
            acc_ref[h] = al_ref[h:h + 1, :] * acc_ref[h] + pv

    @pl.loop(0, n_pairs)
    def _attend_pairs(c2):
        attend(pair(c2), 2, 2 * c2)

    @pl.loop(2 * n_pairs, n_chunks)
    def _attend_rest(c):
        attend(chunk(c), 1, c)
    for h in range(ATT_HEADS):
        o_t = acc_ref[h, :HEAD_DIM, :] / acc_ref[h, HEAD_DIM:HEAD_DIM + 1, :]
        o_ref[:, h * HEAD_DIM:(h + 1) * HEAD_DIM] = o_t.T.astype(o_ref.dtype)


def _dsa(qt, k3, vt4, qit, ki, wt):
    s = ki.shape[0]
    top_k = min(INDEX_TOPK, s // 4)
    resident = pl.Buffered(1)
    return pl.pallas_call(
        functools.partial(_dsa_kernel, top_k=top_k),
        out_shape=jax.ShapeDtypeStruct((s, ATT_WIDTH), BF),
        grid=(s // TQ,),
        in_specs=[pl.BlockSpec((IDX_HEADS * IDX_DIM, TQ), lambda i: (0, i)),
                  pl.BlockSpec((IDX_HEADS, TQ), lambda i: (0, i)),
                  pl.BlockSpec((s, IDX_DIM), lambda i: (0, 0), pipeline_mode=resident),
                  pl.BlockSpec((ATT_HEADS, HEAD_DIM, TQ), lambda i: (0, 0, i)),
                  pl.BlockSpec((ATT_HEADS, s, HEAD_DIM), lambda i: (0, 0, 0), pipeline_mode=resident),
                  pl.BlockSpec((ATT_HEADS, s // KC, V_ROWS, KC), lambda i: (0, 0, 0, 0),
                               pipeline_mode=resident)],
        out_specs=pl.BlockSpec((TQ, ATT_WIDTH), lambda i: (i, 0)),
        scratch_shapes=[pltpu.VMEM((s, TQ), F32),
                        pltpu.VMEM((s, TQ), BF),
                        pltpu.VMEM((ATT_HEADS, TQ), F32),
                        pltpu.VMEM((ATT_HEADS, TQ), F32),
                        pltpu.VMEM((ATT_HEADS, V_ROWS, TQ), F32),
                        pltpu.VMEM((ATT_HEADS, 2 * KC, TQ), F32)],
        compiler_params=_cparams("arbitrary"),
        name="dsa",
    )(qit, wt, ki, qt, k3, vt4)


def _layer_norm(r, g, b):
    mu = jnp.mean(r, -1, keepdims=True)
    d = r - mu
    var = jnp.mean(d * d, -1, keepdims=True)
    return d * lax.rsqrt(var + LN_EPS) * g + b


def _out_ln_kernel(a1_ref, a2_ref, w1_ref, w2_ref, x_ref, g_ref, b_ref, xo_ref, xb_ref, wb_ref):
    @pl.when(pl.program_id(0) == 0)
    def _cast():
        wb_ref[0] = w1_ref[...].astype(wb_ref.dtype)
        wb_ref[1] = w2_ref[...].astype(wb_ref.dtype)

    y = (jnp.dot(a1_ref[...], wb_ref[0], preferred_element_type=F32)
         + jnp.dot(a2_ref[...], wb_ref[1], preferred_element_type=F32))
    o = _layer_norm(DN_ALPHA * x_ref[...] + y, g_ref[...], b_ref[...])
    xo_ref[...] = o
    xb_ref[...] = o.astype(xb_ref.dtype)


def _out_ln(o_gdn, o_dsa, w_out, xf, g, b, layer):
    s = xf.shape[0]
    tm = min(512, s)
    row = lambda i: (i, 0)
    lsel3 = lambda i: (layer, 0, 0)
    return pl.pallas_call(
        _out_ln_kernel,
        out_shape=(jax.ShapeDtypeStruct((s, D_MODEL), F32), jax.ShapeDtypeStruct((s, D_MODEL), BF)),
        grid=(s // tm,),
        in_specs=[pl.BlockSpec((tm, GDN_WIDTH), row), pl.BlockSpec((tm, ATT_WIDTH), row),
                  pl.BlockSpec((None, GDN_WIDTH, D_MODEL), lsel3, pipeline_mode=pl.Buffered(1)),
                  pl.BlockSpec((None, ATT_WIDTH, D_MODEL), lambda i: (layer, 1, 0),
                               pipeline_mode=pl.Buffered(1)),
                  pl.BlockSpec((tm, D_MODEL), row),
                  pl.BlockSpec((None, 1, D_MODEL), lsel3), pl.BlockSpec((None, 1, D_MODEL), lsel3)],
        out_specs=(pl.BlockSpec((tm, D_MODEL), row), pl.BlockSpec((tm, D_MODEL), row)),
        scratch_shapes=[pltpu.VMEM((2, GDN_WIDTH, D_MODEL), BF)],
        compiler_params=_cparams("arbitrary"),
        name="out_ln",
    )(o_gdn, o_dsa, w_out, w_out, xf, g, b)


def _ffn_up_kernel(x_ref, wg_ref, wv_ref, cg_ref, cv_ref, bg_ref, bv_ref, h_ref, carry_ref, wb_ref):
    tm, tn = h_ref.shape

    @pl.when(pl.program_id(1) == 0)
    def _init():
        carry_ref[...] = jnp.zeros_like(carry_ref)
        wb_ref[0] = wg_ref[...].astype(wb_ref.dtype)
        wb_ref[1] = wv_ref[...].astype(wb_ref.dtype)

    row8 = lax.broadcasted_iota(jnp.int32, (SUBLANES, h_ref.shape[1]), 0)

    def conv(u, prev, w, b):
        acc = u * w[FFN_CONV - 1:FFN_CONV, :] + b
        for s in range(1, FFN_CONV):
            us = pltpu.roll(u, s, axis=0)
            top = jnp.where(row8 < s, pltpu.roll(prev, s, axis=0), us[0:SUBLANES, :])
            us = jnp.concatenate([top, us[SUBLANES:, :]], axis=0)
            acc = acc + us * w[FFN_CONV - 1 - s:FFN_CONV - s, :]
        return acc

    x = x_ref[...]
    ug = jnp.dot(x, wb_ref[0], preferred_element_type=F32)
    uv = jnp.dot(x, wb_ref[1], preferred_element_type=F32)
    gate = conv(ug, carry_ref[0], cg_ref[...], bg_ref[...])
    val = conv(uv, carry_ref[1], cv_ref[...], bv_ref[...])
    carry_ref[0] = ug[tm - SUBLANES:, :]
    carry_ref[1] = uv[tm - SUBLANES:, :]
    h_ref[...] = (gate * _sigmoid(gate) * val).astype(h_ref.dtype)


def _ffn_up(xb, w_up, conv_w, conv_b, layer):
    s = xb.shape[0]
    tm = min(1024, s)
    tn = 512
    nj = D_FF // tn
    return pl.pallas_call(
        _ffn_up_kernel,
        out_shape=jax.ShapeDtypeStruct((s, D_FF), BF),
        grid=(nj, s // tm),
        in_specs=[pl.BlockSpec((tm, D_MODEL), lambda j, i: (i, 0)),
                  pl.BlockSpec((None, D_MODEL, tn), lambda j, i: (layer, 0, j)),
                  pl.BlockSpec((None, D_MODEL, tn), lambda j, i: (layer, 0, j + nj)),
                  pl.BlockSpec((None, FFN_CONV, tn), lambda j, i: (layer, 0, j)),
                  pl.BlockSpec((None, FFN_CONV, tn), lambda j, i: (layer, 0, j + nj)),
                  pl.BlockSpec((None, 1, tn), lambda j, i: (layer, 0, j)),
                  pl.BlockSpec((None, 1, tn), lambda j, i: (layer, 0, j + nj))],
        out_specs=pl.BlockSpec((tm, tn), lambda j, i: (i, j)),
        scratch_shapes=[pltpu.VMEM((2, SUBLANES, tn), F32),
                        pltpu.VMEM((2, D_MODEL, tn), BF)],
        compiler_params=_cparams("arbitrary", "arbitrary"),
        name="ffn_up",
    )(xb, w_up, w_up, conv_w, conv_w, conv_b, conv_b)


def _ffn_down_kernel(h_ref, w_ref, x_ref, g_ref, b_ref, xo_ref, xb_ref):
    f = jnp.dot(h_ref[...], w_ref[...], preferred_element_type=F32)
    o = _layer_norm(DN_ALPHA * x_ref[...] + f, g_ref[...], b_ref[...])
    xo_ref[...] = o
    xb_ref[...] = o.astype(xb_ref.dtype)


def _ffn_down_ln(h, w_down, xf, g, b, layer):
    s = xf.shape[0]
    tm = min(256, s)
    row = lambda i: (i, 0)
    lsel3 = lambda i: (layer, 0, 0)
    return pl.pallas_call(
        _ffn_down_kernel,
        out_shape=(jax.ShapeDtypeStruct((s, D_MODEL), F32), jax.ShapeDtypeStruct((s, D_MODEL), BF)),
        grid=(s // tm,),
        in_specs=[pl.BlockSpec((tm, D_FF), row),
                  pl.BlockSpec((None, D_FF, D_MODEL), lsel3, pipeline_mode=pl.Buffered(1)),
                  pl.BlockSpec((tm, D_MODEL), row),
                  pl.BlockSpec((None, 1, D_MODEL), lsel3), pl.BlockSpec((None, 1, D_MODEL), lsel3)],
        out_specs=(pl.BlockSpec((tm, D_MODEL), row), pl.BlockSpec((tm, D_MODEL), row)),
        compiler_params=_cparams("parallel"),
        name="ffn_down",
    )(h, w_down, xf, g, b)


def _rope_tables(seq, dim):
    inv = ROPE_THETA ** (-jnp.arange(0, dim, 2, dtype=F32) / dim)
    ang = jnp.arange(seq, dtype=F32)[:, None] * inv[None, :]
    ang = jnp.concatenate([ang, ang], -1)
    return jnp.cos(ang), jnp.sin(ang)


def _lane_pad(v, offset):
    out = jnp.zeros((v.shape[0], 1, LANES), F32)
    return out.at[:, 0, offset:offset + v.shape[1]].set(v.astype(F32))


def kernel(x, w_in, gdn_conv_w, gdn_a_log, gdn_dt_bias, gdn_norm_w, kv_norm_w, w_ukv, idx_k_norm_g,
           idx_k_norm_b, w_out, ln1_g, ln1_b, ffn_up, ffn_conv_w, ffn_conv_b, ffn_down, ln2_g, ln2_b):
    seq = x.shape[1]
    depth = w_in.shape[0]

    o_ga = MAIN_WIDTH
    o_aq = o_ga + 2 * GDN_HEADS
    o_ckv = o_aq + ATT_WIDTH
    o_iq = o_ckv + KV_RANK
    o_ik = o_iq + IDX_HEADS * IDX_DIM
    o_iw = o_ik + IDX_DIM
    o_end = o_iw + IDX_HEADS
    w_main = w_in[:, :, :o_ga].astype(BF)
    w_rest = w_in[:, :, o_ga:].astype(BF)
    cut = lambda lo, hi: w_rest[:, :, lo - o_ga:hi - o_ga]
    w_tail = jnp.concatenate(
        [cut(o_aq, o_ckv), cut(o_iq, o_ik), cut(o_ckv, o_iq), cut(o_ik, o_end), cut(o_ga, o_aq),
         jnp.zeros((depth, D_MODEL, TAIL_WIDTH - (o_end - o_ga)), BF)], axis=-1)
    wk = w_ukv[:, :, :ATT_WIDTH].astype(BF)
    wvt = jnp.swapaxes(w_ukv[:, :, ATT_WIDTH:], 1, 2).astype(BF)
    w_down_b = ffn_down.astype(BF)

    arow = _lane_pad(gdn_a_log, SM_GA)
    drow = _lane_pad(gdn_dt_bias, SM_GA)
    acol = jnp.swapaxes(arow, 1, 2)
    dcol = jnp.swapaxes(drow, 1, 2)
    ig = _lane_pad(idx_k_norm_g, 0)
    ib = _lane_pad(idx_k_norm_b, 0)
    r3 = lambda v: v.astype(F32)[:, None, :]

    cos, sin = _rope_tables(seq, HEAD_DIM)
    icos, isin = _rope_tables(seq, IDX_DIM)
    sign = jnp.where(jnp.arange(HEAD_DIM) < HEAD_DIM // 2, -1.0, 1.0).astype(F32)
    lane_fill = lambda t: jnp.concatenate([t, jnp.zeros_like(t)], axis=-1)
    tabs = (cos, sin * sign, cos.T, sin.T, icos.T, isin.T, lane_fill(icos), lane_fill(isin))

    masks, tri = _gdn_masks()

    xf = x[0]
    xb = xf.astype(BF)
    for layer in range(depth):
        pm = _proj_main(xb, w_main, layer)
        pt = _proj_tail(xb, w_tail, layer)
        o_gdn = _gdn(pm, pt, gdn_conv_w, arow, acol, drow, dcol, r3(gdn_norm_w), masks, tri, layer)
        qt, k3, vt4, qit, ki, wt = _dsa_prep(pt, wk, wvt, r3(kv_norm_w), ig, ib, tabs, layer)
        o_dsa = _dsa(qt, k3, vt4, qit, ki, wt)
        xf, xb = _out_ln(o_gdn, o_dsa, w_out, xf, r3(ln1_g), r3(ln1_b), layer)
        h = _ffn_up(xb, ffn_up, ffn_conv_w, r3(ffn_conv_b), layer)
        xf, xb = _ffn_down_ln(h, w_down_b, xf, r3(ln2_g), r3(ln2_b), layer)
    return xf[None]
```

```python
import functools

import jax
import jax.numpy as jnp
import numpy as np
from jax import lax
from jax.experimental import pallas as pl
from jax.experimental.pallas import tpu as pltpu

D_MODEL = 2048
DEPTH = 4
HEAD_DIM = 128
GDN_HEADS = 8
ATT_HEADS = 8
GDN_WIDTH = GDN_HEADS * HEAD_DIM
ATT_WIDTH = ATT_HEADS * HEAD_DIM
GDN_CONV = 4
KV_RANK = 256
IDX_HEADS = 16
IDX_DIM = 64
INDEX_TOPK = 256
ROPE_THETA = 10000.0
D_FF = 5632
FFN_CONV = 3
LN_EPS = 1e-5
RMS_EPS = 1e-6
DN_ALPHA = (2 * DEPTH) ** 0.25

BF = jnp.bfloat16
F32 = jnp.float32
HIGHEST = lax.Precision.HIGHEST

LANES = 128
SUBLANES = 8
VMEM_LIMIT_BYTES = 58 * 1024 * 1024

MAIN_WIDTH = 4 * GDN_WIDTH
COL_GDN_QKV = 0
COL_GDN_Z = 3 * GDN_WIDTH
COL_ATT_Q = 0
COL_IDX_Q = COL_ATT_Q + ATT_WIDTH
COL_KV = COL_IDX_Q + IDX_HEADS * IDX_DIM
COL_SMALL = COL_KV + KV_RANK
SM_IW = IDX_DIM
SM_GA = SM_IW + IDX_HEADS
SM_GB = SM_GA + GDN_HEADS
TAIL_WIDTH = COL_SMALL + 2 * LANES

GDN_TILE = 256
INV_BASE = 16
GDN_GROUP = 8
TQ = 256
KC = 256
FFN_SLAB = 128
COUNT_ROWS = 64
UNTESTED_BITS = 24
NEG_BIAS = -1e30
V_ROWS = HEAD_DIM + 16
LOG2E = 1.4426950408889634


def _cparams(*sem):
    return pltpu.CompilerParams(dimension_semantics=sem, vmem_limit_bytes=VMEM_LIMIT_BYTES)


def _sigmoid(x):
    return 1.0 / (1.0 + jnp.exp(-x))


def _softplus(x):
    return jnp.maximum(x, 0.0) + jnp.log(1.0 + jnp.exp(-jnp.abs(x)))


def _dot(a, b):
    return jnp.dot(a.astype(BF), b.astype(BF), preferred_element_type=F32)


def _mm_kernel(a_ref, b_ref, o_ref):
    o_ref[...] = jnp.dot(a_ref[...], b_ref[...], preferred_element_type=F32)


def _proj_main(xb, w_main, layer):
    s, k = xb.shape
    tm = min(1024, s)
    tn = 1024
    return pl.pallas_call(
        _mm_kernel,
        out_shape=jax.ShapeDtypeStruct((s, MAIN_WIDTH), F32),
        grid=(MAIN_WIDTH // tn, s // tm),
        in_specs=[pl.BlockSpec((tm, k), lambda j, i: (i, 0)),
                  pl.BlockSpec((None, k, tn), lambda j, i: (layer, 0, j))],
        out_specs=pl.BlockSpec((tm, tn), lambda j, i: (i, j)),
        compiler_params=_cparams("parallel", "parallel"),
        name="proj_main",
    )(xb, w_main)


def _proj_tail(xb, w_tail, layer):
    s, k = xb.shape
    tm = min(1024, s)
    tn = TAIL_WIDTH // 2
    return pl.pallas_call(
        _mm_kernel,
        out_shape=jax.ShapeDtypeStruct((s, TAIL_WIDTH), F32),
        grid=(TAIL_WIDTH // tn, s // tm),
        in_specs=[pl.BlockSpec((tm, k), lambda j, i: (i, 0)),
                  pl.BlockSpec((None, k, tn), lambda j, i: (layer, 0, j))],
        out_specs=pl.BlockSpec((tm, tn), lambda j, i: (i, j)),
        compiler_params=_cparams("parallel", "parallel"),
        name="proj_tail",
    )(xb, w_tail)


M_LOWER, M_STRICT, M_EYE, M_BASE, M_MERGE0 = 0, 1, 2, 3, 4
N_MERGE = int(np.log2(GDN_TILE // INV_BASE))


def _gdn_masks():
    r = np.arange(GDN_TILE)[:, None]
    c = np.arange(GDN_TILE)[None, :]
    same = lambda size: (r // size) == (c // size)
    rows = [c <= r, c < r, c == r, same(INV_BASE)]
    rows += [same(2 * INV_BASE << i) & ~same(INV_BASE << i) for i in range(N_MERGE)]
    masks = jnp.asarray(np.stack(rows).astype(np.float32))
    tri = jnp.asarray(np.stack([c <= r, r <= c]).astype(np.float32)).astype(BF)
    return masks, tri


def _split3(x):
    h1 = x.astype(BF)
    r1 = x - h1.astype(F32)
    h2 = r1.astype(BF)
    return h1, h2, (r1 - h2.astype(F32)).astype(BF)


def _unit_lower_inverses(a, mask_ref):
    nk = [-(x * mask_ref[M_BASE]) for x in a]
    t = [mask_ref[M_EYE] + x for x in nk]
    for _ in range(int(np.log2(INV_BASE)) - 1):
        nk = [_dot(x, x) for x in nk]
        t = [x + _dot(x, y) for x, y in zip(t, nk)]
    for lvl in range(N_MERGE):
        et = [_dot(x * mask_ref[M_MERGE0 + lvl], y) for x, y in zip(a, t)]
        t = [x - _dot(x, y) for x, y in zip(t, et)]
    return t


def _gdn_kernel(qkv_ref, z_ref, sm_ref, cw_ref, arow_ref, acol_ref, drow_ref, dcol_ref, nw_ref, mask_ref, tri_ref,
                o_ref, carry_ref, state_ref):
    n = GDN_TILE

    @pl.when(pl.program_id(0) == 0)
    def _init():
        carry_ref[...] = jnp.zeros_like(carry_ref)
        state_ref[...] = jnp.zeros_like(state_ref)

    row8 = lax.broadcasted_iota(jnp.int32, (SUBLANES, LANES), 0)

    sm = sm_ref[...]
    g_cols = -jnp.exp(arow_ref[...]) * _softplus(sm + drow_ref[...])
    beta_cols = _sigmoid(sm)
    sm_t = sm.T
    ga = slice(SM_GA, SM_GA + GDN_HEADS)
    g_rows = -jnp.exp(acol_ref[ga, :]) * _softplus(sm_t[ga, :] + dcol_ref[ga, :])
    gc_cols = sum(jnp.dot(tri_ref[0], part, preferred_element_type=F32) for part in _split3(g_cols))
    gc_rows = sum(jnp.dot(part, tri_ref[1], preferred_element_type=F32) for part in _split3(g_rows))

    def conv_silu(off):
        x = qkv_ref[:, off:off + LANES]
        prev = carry_ref[:, off:off + LANES]
        w = cw_ref[:, off:off + LANES]
        acc = x * w[GDN_CONV - 1:GDN_CONV, :]
        for s in range(1, GDN_CONV):
            xs = pltpu.roll(x, s, axis=0)
            top = jnp.where(row8 < s, pltpu.roll(prev, s, axis=0), xs[0:SUBLANES, :])
            xs = jnp.concatenate([top, xs[SUBLANES:, :]], axis=0)
            acc = acc + xs * w[GDN_CONV - 1 - s:GDN_CONV - s, :]
        return acc * _sigmoid(acc)

    def run_heads(hs):
        idx = range(len(hs))
        q = [conv_silu(h * HEAD_DIM) for h in hs]
        k = [conv_silu(GDN_WIDTH + h * HEAD_DIM) for h in hs]
        v = [conv_silu(2 * GDN_WIDTH + h * HEAD_DIM) for h in hs]
        q = [x * lax.rsqrt(jnp.sum(x * x, -1, keepdims=True) + RMS_EPS) * (HEAD_DIM ** -0.5) for x in q]
        k = [x * lax.rsqrt(jnp.sum(x * x, -1, keepdims=True) + RMS_EPS) for x in k]
        gcol = [gc_cols[:, SM_GA + h:SM_GA + h + 1] for h in hs]
        grow = [gc_rows[h:h + 1, :] for h in hs]
        bcol = [beta_cols[:, SM_GB + h:SM_GB + h + 1] for h in hs]
        glast = [x[n - 1:n, :] for x in gcol]
        ecol = [jnp.exp(x) for x in gcol]
        decay = [mask_ref[M_LOWER] * jnp.exp(jnp.minimum(c - r, 0.0)) for c, r in zip(gcol, grow)]
        k_t = [x.T for x in k]
        k_tb = [x.astype(BF) for x in k_t]
        kk = [jnp.dot(x.astype(BF), y, preferred_element_type=F32) for x, y in zip(k, k_tb)]
        qk = [jnp.dot(x.astype(BF), y, preferred_element_type=F32) for x, y in zip(q, k_tb)]
        a = [mask_ref[M_STRICT] * (b * x * d) for b, x, d in zip(bcol, kk, decay)]
        t = _unit_lower_inverses(a, mask_ref)
        rhs = [jnp.concatenate([v[i] * bcol[i], k[i] * (bcol[i] * ecol[i])], axis=1) for i in idx]
        sol = [_dot(x, y) for x, y in zip(t, rhs)]
        state = [state_ref[h] for h in hs]
        state_b = [x.astype(BF) for x in state]
        v_new = [x[:, :HEAD_DIM] - jnp.dot(x[:, HEAD_DIM:].astype(BF), s, preferred_element_type=F32)
                 for x, s in zip(sol, state_b)]
        v_nb = [x.astype(BF) for x in v_new]
        o = [jnp.dot((q[i] * ecol[i]).astype(BF), state_b[i], preferred_element_type=F32)
             + jnp.dot((qk[i] * decay[i]).astype(BF), v_nb[i], preferred_element_type=F32) for i in idx]
        for i, h in enumerate(hs):
            k_dec_t = k_t[i] * jnp.exp(glast[i] - grow[i])
            state_ref[h] = state[i] * jnp.exp(glast[i]) + jnp.dot(k_dec_t.astype(BF), v_nb[i],
                                                                  preferred_element_type=F32)
        for i, h in enumerate(hs):
            on = o[i] * lax.rsqrt(jnp.mean(o[i] * o[i], -1, keepdims=True) + RMS_EPS) * nw_ref[...]
            z = z_ref[:, h * HEAD_DIM:(h + 1) * HEAD_DIM]
            o_ref[:, h * HEAD_DIM:(h + 1) * HEAD_DIM] = (on * (z * _sigmoid(z))).astype(o_ref.dtype)

    for g in range(0, GDN_HEADS, GDN_GROUP):
        run_heads(list(range(g, g + GDN_GROUP)))

    carry_ref[...] = qkv_ref[n - SUBLANES:n, :]


def _gdn(pm, pt, conv_w, arow, acol, drow, dcol, norm_w, masks, tri, layer):
    s = pm.shape[0]
    n = GDN_TILE
    lsel3 = lambda i: (layer, 0, 0)
    const3 = lambda i: (0, 0, 0)
    return pl.pallas_call(
        _gdn_kernel,
        out_shape=jax.ShapeDtypeStruct((s, GDN_WIDTH), BF),
        grid=(s // n,),
        in_specs=[pl.BlockSpec((n, 3 * GDN_WIDTH), lambda i: (i, COL_GDN_QKV // (3 * GDN_WIDTH))),
                  pl.BlockSpec((n, GDN_WIDTH), lambda i: (i, COL_GDN_Z // GDN_WIDTH)),
                  pl.BlockSpec((n, LANES), lambda i: (i, COL_SMALL // LANES)),
                  pl.BlockSpec((None, GDN_CONV, 3 * GDN_WIDTH), lsel3),
                  pl.BlockSpec((None, 1, LANES), lsel3),
                  pl.BlockSpec((None, LANES, 1), lsel3),
                  pl.BlockSpec((None, 1, LANES), lsel3),
                  pl.BlockSpec((None, LANES, 1), lsel3),
                  pl.BlockSpec((None, 1, HEAD_DIM), lsel3),
                  pl.BlockSpec(masks.shape, const3, pipeline_mode=pl.Buffered(1)),
                  pl.BlockSpec(tri.shape, const3, pipeline_mode=pl.Buffered(1))],
        out_specs=pl.BlockSpec((n, GDN_WIDTH), lambda i: (i, 0)),
        scratch_shapes=[pltpu.VMEM((SUBLANES, 3 * GDN_WIDTH), F32),
                        pltpu.VMEM((GDN_HEADS, HEAD_DIM, HEAD_DIM), F32)],
        compiler_params=_cparams("arbitrary"),
        name="gdn",
    )(pm, pm, pt, conv_w, arow, acol, drow, dcol, norm_w, masks, tri)


def _dsa_prep_kernel(aq_ref, iq_ref, ckv_ref, sm_ref, wk_ref, wvt_ref, kvg_ref, ig_ref, ib_ref,
                     cos_ref, sins_ref, cos_t_ref, sin_t_ref, icos_t_ref, isin_t_ref, icos_ref, isin_ref,
                     qt_ref, k_ref, vt_ref, qit_ref, ki_ref, wt_ref):
    half = HEAD_DIM // 2
    ihalf = IDX_DIM // 2

    aq_t = aq_ref[...].T
    cos_t = cos_t_ref[...]
    sin_t = sin_t_ref[...]
    for h in range(ATT_HEADS):
        x = aq_t[h * HEAD_DIM:(h + 1) * HEAD_DIM, :]
        rot = jnp.concatenate([-x[half:, :], x[:half, :]], axis=0)
        qt_ref[h] = ((x * cos_t + rot * sin_t) * (LOG2E * HEAD_DIM ** -0.5)).astype(qt_ref.dtype)

    iq_t = iq_ref[...].T
    icos_t = icos_t_ref[...]
    isin_t = isin_t_ref[...]
    for h in range(IDX_HEADS):
        x = iq_t[h * IDX_DIM:(h + 1) * IDX_DIM, :]
        rot = jnp.concatenate([-x[ihalf:, :], x[:ihalf, :]], axis=0)
        qit_ref[h * IDX_DIM:(h + 1) * IDX_DIM, :] = (
            (x * icos_t + rot * isin_t) * (IDX_DIM ** -0.5)).astype(qit_ref.dtype)

    c = ckv_ref[...]
    kvn = c * lax.rsqrt(jnp.mean(c * c, -1, keepdims=True) + RMS_EPS) * kvg_ref[...]
    k = jnp.dot(kvn.astype(BF), wk_ref[...], preferred_element_type=F32)
    cos = cos_ref[...]
    sins = sins_ref[...]
    for h in range(ATT_HEADS):
        x = k[:, h * HEAD_DIM:(h + 1) * HEAD_DIM]
        k_ref[h] = (x * cos + pltpu.roll(x, half, axis=1) * sins).astype(k_ref.dtype)
    v_t = jnp.dot(wvt_ref[...], kvn.T.astype(BF), preferred_element_type=F32)
    for h in range(ATT_HEADS):
        vt_ref[h, 0, :HEAD_DIM, :] = v_t[h * HEAD_DIM:(h + 1) * HEAD_DIM, :].astype(vt_ref.dtype)
        vt_ref[h, 0, HEAD_DIM:, :] = jnp.ones((V_ROWS - HEAD_DIM, v_t.shape[1]), vt_ref.dtype)

    sm = sm_ref[...]
    lane = lax.broadcasted_iota(jnp.int32, sm.shape, 1)
    is_k = lane < IDX_DIM
    mu = jnp.sum(jnp.where(is_k, sm, 0.0), -1, keepdims=True) * (1.0 / IDX_DIM)
    d = jnp.where(is_k, sm - mu, 0.0)
    var = jnp.sum(d * d, -1, keepdims=True) * (1.0 / IDX_DIM)
    kin = d * lax.rsqrt(var + LN_EPS) * ig_ref[...] + ib_ref[...]
    below = pltpu.roll(kin, ihalf, axis=1)
    above = pltpu.roll(kin, LANES - ihalf, axis=1)
    rot = jnp.where((lane & (IDX_DIM - 1)) < ihalf, -above, below)
    kir = kin * icos_ref[...] + rot * isin_ref[...]
    ki_ref[...] = kir[:, :IDX_DIM].astype(ki_ref.dtype)
    wt_ref[...] = sm.T[SM_IW:SM_IW + IDX_HEADS, :] * (IDX_HEADS ** -0.5)


def _dsa_prep(p, wk, wvt, kvg, ig, ib, tabs, layer):
    s = p.shape[0]
    tp = KC
    lsel3 = lambda i: (layer, 0, 0)
    row = lambda i: (i, 0)
    col = lambda i: (0, i)
    out_shape = (jax.ShapeDtypeStruct((ATT_HEADS, HEAD_DIM, s), BF),
                 jax.ShapeDtypeStruct((ATT_HEADS, s, HEAD_DIM), BF),
                 jax.ShapeDtypeStruct((ATT_HEADS, s // tp, V_ROWS, tp), BF),
                 jax.ShapeDtypeStruct((IDX_HEADS * IDX_DIM, s), BF),
                 jax.ShapeDtypeStruct((s, IDX_DIM), BF),
                 jax.ShapeDtypeStruct((IDX_HEADS, s), F32))
    return pl.pallas_call(
        _dsa_prep_kernel,
        out_shape=out_shape,
        grid=(s // tp,),
        in_specs=[pl.BlockSpec((tp, ATT_WIDTH), lambda i: (i, COL_ATT_Q // ATT_WIDTH)),
                  pl.BlockSpec((tp, IDX_HEADS * IDX_DIM), lambda i: (i, COL_IDX_Q // (IDX_HEADS * IDX_DIM))),
                  pl.BlockSpec((tp, KV_RANK), lambda i: (i, COL_KV // KV_RANK)),
                  pl.BlockSpec((tp, LANES), lambda i: (i, COL_SMALL // LANES)),
                  pl.BlockSpec((None, KV_RANK, ATT_WIDTH), lsel3),
                  pl.BlockSpec((None, ATT_WIDTH, KV_RANK), lsel3),
                  pl.BlockSpec((None, 1, KV_RANK), lsel3),
                  pl.BlockSpec((None, 1, LANES), lsel3),
                  pl.BlockSpec((None, 1, LANES), lsel3),
                  pl.BlockSpec((tp, HEAD_DIM), row), pl.BlockSpec((tp, HEAD_DIM), row),
                  pl.BlockSpec((HEAD_DIM, tp), col), pl.BlockSpec((HEAD_DIM, tp), col),
                  pl.BlockSpec((IDX_DIM, tp), col), pl.BlockSpec((IDX_DIM, tp), col),
                  pl.BlockSpec((tp, LANES), row), pl.BlockSpec((tp, LANES), row)],
        out_specs=(pl.BlockSpec((ATT_HEADS, HEAD_DIM, tp), lambda i: (0, 0, i)),
                   pl.BlockSpec((ATT_HEADS, tp, HEAD_DIM), lambda i: (0, i, 0)),
                   pl.BlockSpec((ATT_HEADS, 1, V_ROWS, tp), lambda i: (0, i, 0, 0)),
                   pl.BlockSpec((IDX_HEADS * IDX_DIM, tp), col),
                   pl.BlockSpec((tp, IDX_DIM), row),
                   pl.BlockSpec((IDX_HEADS, tp), col)),
        compiler_params=_cparams("parallel"),
        name="dsa_prep",
    )(p, p, p, p, wk, wvt, kvg, ig, ib, *tabs)


def _dsa_kernel(qit_ref, wt_ref, ki_ref, qt_ref, k_ref, vt_ref, o_ref, sc_ref, sh_ref, m_ref, al_ref, acc_ref,
                s_ref, *, top_k):
    i = pl.program_id(0)
    n_chunks = i + 1
    n_pairs = lax.shift_right_logical(n_chunks, 1)

    def chunk(c):
        return pl.ds(pl.multiple_of(c * KC, KC), KC)

    def pair(c2):
        return pl.ds(pl.multiple_of(c2 * (2 * KC), 2 * KC), 2 * KC)

    def high_half(v):
        return lax.bitcast_convert_type(lax.bitcast_convert_type(v, jnp.int32) & jnp.int32(-2 ** 16), F32)

    w_t = wt_ref[...]

    def score_rows(rows, n_rows, first_key):
        kic = ki_ref[rows, :]
        acc = jnp.zeros((n_rows, TQ), F32)
        for h in range(IDX_HEADS):
            d = jnp.dot(kic, qit_ref[h * IDX_DIM:(h + 1) * IDX_DIM, :], preferred_element_type=F32)
            acc = acc + w_t[h:h + 1, :] * jnp.maximum(d, 0.0)
        visible = (first_key + lax.broadcasted_iota(jnp.int32, (n_rows, TQ), 0)
                   <= i * TQ + lax.broadcasted_iota(jnp.int32, (n_rows, TQ), 1))
        score = jnp.where(visible, acc, jnp.nan)
        sc_ref[rows, :] = score
        sh_ref[rows, :] = high_half(score).astype(sh_ref.dtype)

    n_quads = lax.shift_right_logical(n_chunks, 2)

    @pl.loop(0, n_quads)
    def _score_quads(c4):
        score_rows(pl.ds(pl.multiple_of(c4 * (4 * KC), 4 * KC), 4 * KC), 4 * KC, c4 * (4 * KC))

    @pl.loop(2 * n_quads, n_pairs)
    def _score_pairs(c2):
        score_rows(pair(c2), 2 * KC, c2 * (2 * KC))

    @pl.loop(2 * n_pairs, n_chunks)
    def _score_rest(c):
        score_rows(chunk(c), KC, c * KC)

    def count(pred):
        def part(rows):
            one = jnp.where(pred(sc_ref[rows, :]), 1.0, 0.0)
            return jnp.sum(one.reshape(-1, COUNT_ROWS, TQ), axis=0)

        cnt = lax.fori_loop(0, n_pairs, lambda c2, cnt: cnt + part(pair(c2)), jnp.zeros((COUNT_ROWS, TQ), F32))
        cnt = lax.fori_loop(2 * n_pairs, n_chunks, lambda c, cnt: cnt + part(chunk(c)), cnt)
        return jnp.sum(cnt, axis=0, keepdims=True)

    def count_high(thr_h):
        def part(rows):
            one = jnp.where(sh_ref[rows, :] >= thr_h, jnp.ones((), BF), jnp.zeros((), BF))
            one = one.reshape(-1, COUNT_ROWS, TQ)
            tot = one[0]
            for g in range(1, one.shape[0]):
                tot = tot + one[g]
            return tot

        cnt = lax.fori_loop(0, n_pairs, lambda c2, cnt: cnt + part(pair(c2)), jnp.zeros((COUNT_ROWS, TQ), BF))
        cnt = lax.fori_loop(2 * n_pairs, n_chunks, lambda c, cnt: cnt + part(chunk(c)), cnt)
        return jnp.sum(cnt.astype(F32), axis=0, keepdims=True)

    def key_to_f32(cu):
        ks = cu ^ jnp.int32(-2 ** 31)
        bits = jnp.where(ks < 0, ks ^ jnp.int32(2 ** 31 - 1), ks)
        return lax.bitcast_convert_type(bits, F32)

    def bit_pass(it, st, high=False):
        cu, done = st
        cand = cu | lax.shift_left(jnp.int32(1), 31 - it)
        thr_c = key_to_f32(cand)
        cnt = count_high(high_half(thr_c).astype(BF)) if high else count(lambda x: x >= thr_c)
        take = jnp.logical_and(cnt >= top_k, done == 0)
        return jnp.where(take, cand, cu), jnp.where(jnp.logical_and(take, cnt == top_k), 1, done)

    def bits_left(st):
        it, _, done = st
        return jnp.logical_and(it < 32, jnp.min(done) == 0)

    def two_passes(st):
        it, cu, done = st
        cu, done = lax.fori_loop(it, it + 2, bit_pass, (cu, done))
        return it + 2, cu, done

    st = lax.fori_loop(0, 16, functools.partial(bit_pass, high=True),
                       (jnp.zeros((1, TQ), jnp.int32), jnp.zeros((1, TQ), jnp.int32)))
    st = lax.fori_loop(16, UNTESTED_BITS, bit_pass, st)
    _, cu, done = lax.while_loop(bits_left, two_passes, (jnp.int32(UNTESTED_BITS),) + st)
    thr = jnp.where((cu & jnp.int32(-2 ** 23)) == 0, -jnp.inf, key_to_f32(cu))
    has_tie = lax.cond(jnp.min(done) == 1, lambda: jnp.bool_(False),
                       lambda: jnp.max(count(lambda x: x >= thr)) > top_k)

    @pl.when(jnp.logical_not(has_tie))
    def _select():
        def body(c, carry):
            sc_ref[chunk(c), :] = jnp.where(sc_ref[chunk(c), :] >= thr, 0.0, NEG_BIAS)
            return carry
        lax.fori_loop(0, n_chunks, body, 0)

    @pl.when(has_tie)
    def _select_ties():
        need = top_k - count(lambda x: x > thr)
        ltri = jnp.where(lax.broadcasted_iota(jnp.int32, (KC, KC), 1)
                         <= lax.broadcasted_iota(jnp.int32, (KC, KC), 0), 1.0, 0.0).astype(BF)

        def body(c, seen):
            x = sc_ref[chunk(c), :]
            eq = x == thr
            rank = jnp.dot(ltri, jnp.where(eq, 1.0, 0.0).astype(BF), preferred_element_type=F32) + seen
            keep = jnp.logical_or(x > thr, jnp.logical_and(eq, rank <= need))
            sc_ref[chunk(c), :] = jnp.where(keep, 0.0, NEG_BIAS)
            return rank[KC - 1:KC, :]
        lax.fori_loop(0, n_chunks, body, jnp.zeros((1, TQ), F32))

    m_ref[...] = jnp.full(m_ref.shape, -jnp.inf, F32)
    acc_ref[...] = jnp.zeros_like(acc_ref)

    def attend(rows, n_sub, first_chunk):
        n_rows = n_sub * KC
        for h in range(ATT_HEADS):
            s = jnp.dot(k_ref[h, rows, :], qt_ref[h], preferred_element_type=F32) + sc_ref[rows, :]
            s_ref[h, :n_rows, :] = s
            m_old = m_ref[h:h + 1, :]
            m_new = jnp.maximum(m_old, jnp.max(s, axis=0, keepdims=True))
            al_ref[h:h + 1, :] = jnp.exp2(m_old - m_new)
            m_ref[h:h + 1, :] = m_new
        for h in range(ATT_HEADS):
            p = jnp.exp2(s_ref[h, :n_rows, :] - m_ref[h:h + 1, :]).astype(BF)
            pv = jnp.dot(vt_ref[h, first_chunk], p[:KC, :], preferred_element_type=F32)
            for j in range(1, n_sub):
                pv = pv + jnp.dot(vt_ref[h, first_chunk + j], p[j * KC:(j + 1) * KC, :],
                                  preferred_element_type=F32)
            acc_ref[h] = al_ref[h:h + 1, :] * acc_ref[h] + pv

    @pl.loop(0, n_pairs)
    def _attend_pairs(c2):
        attend(pair(c2), 2, 2 * c2)

    @pl.loop(2 * n_pairs, n_chunks)
    def _attend_rest(c):
        attend(chunk(c), 1, c)
    for h in range(ATT_HEADS):
        o_t = acc_ref[h, :HEAD_DIM, :] / acc_ref[h, HEAD_DIM:HEAD_DIM + 1, :]
        o_ref[:, h * HEAD_DIM:(h + 1) * HEAD_DIM] = o_t.T.astype(o_ref.dtype)


def _dsa(qt, k3, vt4, qit, ki, wt):
    s = ki.shape[0]
    top_k = min(INDEX_TOPK, s // 4)
    resident = pl.Buffered(1)
    return pl.pallas_call(
        functools.partial(_dsa_kernel, top_k=top_k),
        out_shape=jax.ShapeDtypeStruct((s, ATT_WIDTH), BF),
        grid=(s // TQ,),
        in_specs=[pl.BlockSpec((IDX_HEADS * IDX_DIM, TQ), lambda i: (0, i)),
                  pl.BlockSpec((IDX_HEADS, TQ), lambda i: (0, i)),
                  pl.BlockSpec((s, IDX_DIM), lambda i: (0, 0), pipeline_mode=resident),
                  pl.BlockSpec((ATT_HEADS, HEAD_DIM, TQ), lambda i: (0, 0, i)),
                  pl.BlockSpec((ATT_HEADS, s, HEAD_DIM), lambda i: (0, 0, 0), pipeline_mode=resident),
                  pl.BlockSpec((ATT_HEADS, s // KC, V_ROWS, KC), lambda i: (0, 0, 0, 0),
                               pipeline_mode=resident)],
        out_specs=pl.BlockSpec((TQ, ATT_WIDTH), lambda i: (i, 0)),
        scratch_shapes=[pltpu.VMEM((s, TQ), F32),
                        pltpu.VMEM((s, TQ), BF),
                        pltpu.VMEM((ATT_HEADS, TQ), F32),
                        pltpu.VMEM((ATT_HEADS, TQ), F32),
                        pltpu.VMEM((ATT_HEADS, V_ROWS, TQ), F32),
                        pltpu.VMEM((ATT_HEADS, 2 * KC, TQ), F32)],
        compiler_params=_cparams("arbitrary"),
        name="dsa",
    )(qit, wt, ki, qt, k3, vt4)


def _layer_norm(r, g, b):
    mu = jnp.mean(r, -1, keepdims=True)
    d = r - mu
    var = jnp.mean(d * d, -1, keepdims=True)
    return d * lax.rsqrt(var + LN_EPS) * g + b


def _out_ln_kernel(a1_ref, a2_ref, w1_ref, w2_ref, x_ref, g_ref, b_ref, xo_ref, xb_ref, wb_ref):
    @pl.when(pl.program_id(0) == 0)
    def _cast():
        wb_ref[0] = w1_ref[...].astype(wb_ref.dtype)
        wb_ref[1] = w2_ref[...].astype(wb_ref.dtype)

    y = (jnp.dot(a1_ref[...], wb_ref[0], preferred_element_type=F32)
         + jnp.dot(a2_ref[...], wb_ref[1], preferred_element_type=F32))
    o = _layer_norm(DN_ALPHA * x_ref[...] + y, g_ref[...], b_ref[...])
    xo_ref[...] = o
    xb_ref[...] = o.astype(xb_ref.dtype)


def _out_ln(o_gdn, o_dsa, w_out, xf, g, b, layer):
    s = xf.shape[0]
    tm = min(512, s)
    row = lambda i: (i, 0)
    lsel3 = lambda i: (layer, 0, 0)
    return pl.pallas_call(
        _out_ln_kernel,
        out_shape=(jax.ShapeDtypeStruct((s, D_MODEL), F32), jax.ShapeDtypeStruct((s, D_MODEL), BF)),
        grid=(s // tm,),
        in_specs=[pl.BlockSpec((tm, GDN_WIDTH), row), pl.BlockSpec((tm, ATT_WIDTH), row),
                  pl.BlockSpec((None, GDN_WIDTH, D_MODEL), lsel3, pipeline_mode=pl.Buffered(1)),
                  pl.BlockSpec((None, ATT_WIDTH, D_MODEL), lambda i: (layer, 1, 0),
                               pipeline_mode=pl.Buffered(1)),
                  pl.BlockSpec((tm, D_MODEL), row),
                  pl.BlockSpec((None, 1, D_MODEL), lsel3), pl.BlockSpec((None, 1, D_MODEL), lsel3)],
        out_specs=(pl.BlockSpec((tm, D_MODEL), row), pl.BlockSpec((tm, D_MODEL), row)),
        scratch_shapes=[pltpu.VMEM((2, GDN_WIDTH, D_MODEL), BF)],
        compiler_params=_cparams("arbitrary"),
        name="out_ln",
    )(o_gdn, o_dsa, w_out, w_out, xf, g, b)


def _ffn_up_kernel(x_ref, wg_ref, wv_ref, cg_ref, cv_ref, bg_ref, bv_ref, h_ref, carry_ref, wb_ref):
    tm, tn = h_ref.shape

    @pl.when(pl.program_id(1) == 0)
    def _init():
        carry_ref[...] = jnp.zeros_like(carry_ref)
        wb_ref[0] = wg_ref[...].astype(wb_ref.dtype)
        wb_ref[1] = wv_ref[...].astype(wb_ref.dtype)

    row8 = lax.broadcasted_iota(jnp.int32, (SUBLANES, h_ref.shape[1]), 0)

    def conv(u, prev, w, b):
        acc = u * w[FFN_CONV - 1:FFN_CONV, :] + b
        for s in range(1, FFN_CONV):
            us = pltpu.roll(u, s, axis=0)
            top = jnp.where(row8 < s, pltpu.roll(prev, s, axis=0), us[0:SUBLANES, :])
            us = jnp.concatenate([top, us[SUBLANES:, :]], axis=0)
            acc = acc + us * w[FFN_CONV - 1 - s:FFN_CONV - s, :]
        return acc

    x = x_ref[...]
    ug = jnp.dot(x, wb_ref[0], preferred_element_type=F32)
    uv = jnp.dot(x, wb_ref[1], preferred_element_type=F32)
    prev_g = carry_ref[0]
    prev_v = carry_ref[1]
    for r in range(0, tm, FFN_SLAB):
        sg = ug[r:r + FFN_SLAB, :]
        sv = uv[r:r + FFN_SLAB, :]
        gate = conv(sg, prev_g, cg_ref[...], bg_ref[...])
        val = conv(sv, prev_v, cv_ref[...], bv_ref[...])
        h_ref[r:r + FFN_SLAB, :] = (gate * _sigmoid(gate) * val).astype(h_ref.dtype)
        prev_g = sg[FFN_SLAB - SUBLANES:, :]
        prev_v = sv[FFN_SLAB - SUBLANES:, :]
    carry_ref[0] = prev_g
    carry_ref[1] = prev_v


def _ffn_up(xb, w_up, conv_w, conv_b, layer):
    s = xb.shape[0]
    tm = min(1024, s)
    tn = 512
    nj = D_FF // tn
    return pl.pallas_call(
        _ffn_up_kernel,
        out_shape=jax.ShapeDtypeStruct((s, D_FF), BF),
        grid=(nj, s // tm),
        in_specs=[pl.BlockSpec((tm, D_MODEL), lambda j, i: (i, 0)),
                  pl.BlockSpec((None, D_MODEL, tn), lambda j, i: (layer, 0, j)),
                  pl.BlockSpec((None, D_MODEL, tn), lambda j, i: (layer, 0, j + nj)),
                  pl.BlockSpec((None, FFN_CONV, tn), lambda j, i: (layer, 0, j)),
                  pl.BlockSpec((None, FFN_CONV, tn), lambda j, i: (layer, 0, j + nj)),
                  pl.BlockSpec((None, 1, tn), lambda j, i: (layer, 0, j)),
                  pl.BlockSpec((None, 1, tn), lambda j, i: (layer, 0, j + nj))],
        out_specs=pl.BlockSpec((tm, tn), lambda j, i: (i, j)),
        scratch_shapes=[pltpu.VMEM((2, SUBLANES, tn), F32),
                        pltpu.VMEM((2, D_MODEL, tn), BF)],
        compiler_params=_cparams("arbitrary", "arbitrary"),
        name="ffn_up",
    )(xb, w_up, w_up, conv_w, conv_w, conv_b, conv_b)


def _ffn_down_kernel(h_ref, w_ref, x_ref, g_ref, b_ref, xo_ref, xb_ref):
    f = jnp.dot(h_ref[...], w_ref[...], preferred_element_type=F32)
    o = _layer_norm(DN_ALPHA * x_ref[...] + f, g_ref[...], b_ref[...])
    xo_ref[...] = o
    xb_ref[...] = o.astype(xb_ref.dtype)


def _ffn_down_ln(h, w_down, xf, g, b, layer):
    s = xf.shape[0]
    tm = min(256, s)
    row = lambda i: (i, 0)
    lsel3 = lambda i: (layer, 0, 0)
    return pl.pallas_call(
        _ffn_down_kernel,
        out_shape=(jax.ShapeDtypeStruct((s, D_MODEL), F32), jax.ShapeDtypeStruct((s, D_MODEL), BF)),
        grid=(s // tm,),
        in_specs=[pl.BlockSpec((tm, D_FF), row),
                  pl.BlockSpec((None, D_FF, D_MODEL), lsel3, pipeline_mode=pl.Buffered(1)),
                  pl.BlockSpec((tm, D_MODEL), row),
                  pl.BlockSpec((None, 1, D_MODEL), lsel3), pl.BlockSpec((None, 1, D_MODEL), lsel3)],
        out_specs=(pl.BlockSpec((tm, D_MODEL), row), pl.BlockSpec((tm, D_MODEL), row)),
        compiler_params=_cparams("parallel"),
        name="ffn_down",
    )(h, w_down, xf, g, b)


def _rope_tables(seq, dim):
    inv = ROPE_THETA ** (-jnp.arange(0, dim, 2, dtype=F32) / dim)
    ang = jnp.arange(seq, dtype=F32)[:, None] * inv[None, :]
    ang = jnp.concatenate([ang, ang], -1)
    return jnp.cos(ang), jnp.sin(ang)


def _lane_pad(v, offset):
    out = jnp.zeros((v.shape[0], 1, LANES), F32)
    return out.at[:, 0, offset:offset + v.shape[1]].set(v.astype(F32))


def kernel(x, w_in, gdn_conv_w, gdn_a_log, gdn_dt_bias, gdn_norm_w, kv_norm_w, w_ukv, idx_k_norm_g,
           idx_k_norm_b, w_out, ln1_g, ln1_b, ffn_up, ffn_conv_w, ffn_conv_b, ffn_down, ln2_g, ln2_b):
    seq = x.shape[1]
    depth = w_in.shape[0]

    o_ga = MAIN_WIDTH
    o_aq = o_ga + 2 * GDN_HEADS
    o_ckv = o_aq + ATT_WIDTH
    o_iq = o_ckv + KV_RANK
    o_ik = o_iq + IDX_HEADS * IDX_DIM
    o_iw = o_ik + IDX_DIM
    o_end = o_iw + IDX_HEADS
    w_main = w_in[:, :, :o_ga].astype(BF)
    w_rest = w_in[:, :, o_ga:].astype(BF)
    cut = lambda lo, hi: w_rest[:, :, lo - o_ga:hi - o_ga]
    w_tail = jnp.concatenate(
        [cut(o_aq, o_ckv), cut(o_iq, o_ik), cut(o_ckv, o_iq), cut(o_ik, o_end), cut(o_ga, o_aq),
         jnp.zeros((depth, D_MODEL, TAIL_WIDTH - (o_end - o_ga)), BF)], axis=-1)
    wk = w_ukv[:, :, :ATT_WIDTH].astype(BF)
    wvt = jnp.swapaxes(w_ukv[:, :, ATT_WIDTH:], 1, 2).astype(BF)
    w_down_b = ffn_down.astype(BF)

    arow = _lane_pad(gdn_a_log, SM_GA)
    drow = _lane_pad(gdn_dt_bias, SM_GA)
    acol = jnp.swapaxes(arow, 1, 2)
    dcol = jnp.swapaxes(drow, 1, 2)
    ig = _lane_pad(idx_k_norm_g, 0)
    ib = _lane_pad(idx_k_norm_b, 0)
    r3 = lambda v: v.astype(F32)[:, None, :]

    cos, sin = _rope_tables(seq, HEAD_DIM)
    icos, isin = _rope_tables(seq, IDX_DIM)
    sign = jnp.where(jnp.arange(HEAD_DIM) < HEAD_DIM // 2, -1.0, 1.0).astype(F32)
    lane_fill = lambda t: jnp.concatenate([t, jnp.zeros_like(t)], axis=-1)
    tabs = (cos, sin * sign, cos.T, sin.T, icos.T, isin.T, lane_fill(icos), lane_fill(isin))

    masks, tri = _gdn_masks()

    xf = x[0]
    xb = xf.astype(BF)
    for layer in range(depth):
        pm = _proj_main(xb, w_main, layer)
        pt = _proj_tail(xb, w_tail, layer)
        o_gdn = _gdn(pm, pt, gdn_conv_w, arow, acol, drow, dcol, r3(gdn_norm_w), masks, tri, layer)
        qt, k3, vt4, qit, ki, wt = _dsa_prep(pt, wk, wvt, r3(kv_norm_w), ig, ib, tabs, layer)
        o_dsa = _dsa(qt, k3, vt4, qit, ki, wt)
        xf, xb = _out_ln(o_gdn, o_dsa, w_out, xf, r3(ln1_g), r3(ln1_b), layer)
        h = _ffn_up(xb, ffn_up, ffn_conv_w, r3(ffn_conv_b), layer)
        xf, xb = _ffn_down_ln(h, w_down_b, xf, r3(ln2_g), r3(ln2_b), layer)
    return xf[None]
```
